```python
import math
import jax, jax.numpy as jnp
from jax import lax
import numpy as np

D_MODEL = 1024
BATCH = 8
SEQ = 4096
DEPTH = 2

N_META = 16
EPS = 1e-6
A_HEADS = 4
A_WIDTH = D_MODEL
A_HEAD_DIM = A_WIDTH // A_HEADS
A_CHUNK = 64
B_WIDTH = D_MODEL
B_BLOCKS = 8
B_BLOCK_DIM = B_WIDTH // B_BLOCKS
B_CONV = 4
LRU_C = 8.0
C_HEADS = D_MODEL // 128
C_QK_DIM = D_MODEL // (2 * C_HEADS)
C_V_DIM = 2 * C_QK_DIM
ROT_DIM = C_QK_DIM // 4
ROPE_THETA = 500000.0
Q_BLOCK = 128
D_FF = 4 * D_MODEL
N_AB = (DEPTH + 1) // 2
N_C = DEPTH // 2
AB_SPLITS = [A_WIDTH, A_WIDTH, A_WIDTH, A_WIDTH, A_HEADS, A_HEADS, B_WIDTH, B_WIDTH]
AB_IN = sum(AB_SPLITS)
C_SPLITS = [C_HEADS * 2 * C_QK_DIM, C_HEADS * 2 * C_QK_DIM, C_HEADS * C_V_DIM]
C_IN = sum(C_SPLITS)

kernel_name = "hybrid_mlstm_rglru_diffattn_trunk"


def _rmsnorm(x, g):
    xf = x.astype(jnp.float32)
    y = xf * lax.rsqrt(jnp.mean(xf * xf, axis=-1, keepdims=True) + EPS)
    return (y * g.astype(jnp.float32)).astype(x.dtype)


def _split(a, sizes):
    return jnp.split(a, np.cumsum(sizes)[:-1].tolist(), axis=-1)


def _mlstm_chunk(state, inp):
    c_mem, n_mem, m_prev = state
    q, k, v, log_i, log_f = inp
    L = q.shape[2]
    b = jnp.cumsum(log_f, axis=-1)
    causal = jnp.tril(jnp.ones((L, L), dtype=bool))
    log_d = b[..., :, None] - b[..., None, :] + log_i[..., None, :]
    log_d = jnp.where(causal, log_d, -jnp.inf)
    log_prev = b + m_prev[..., None]
    m_t = jnp.maximum(log_prev, jnp.max(log_d, axis=-1))
    d = jnp.exp(log_d - m_t[..., None])
    w_prev = jnp.exp(log_prev - m_t)
    s = jnp.einsum('bhtd,bhsd->bhts', q, k) * d
    num = (w_prev[..., None] * jnp.einsum('bhvk,bhtk->bhtv', c_mem, q)
           + jnp.einsum('bhts,bhsv->bhtv', s, v))
    den = w_prev * jnp.einsum('bhk,bhtk->bht', n_mem, q) + jnp.sum(s, axis=-1)
    h = num / jnp.maximum(jnp.abs(den), jnp.exp(-m_t))[..., None]
    log_end = b[..., -1:] - b + log_i
    m_new = jnp.maximum(b[..., -1] + m_prev, jnp.max(log_end, axis=-1))
    w_s = jnp.exp(log_end - m_new[..., None])
    decay = jnp.exp(b[..., -1] + m_prev - m_new)
    c_new = decay[..., None, None] * c_mem + jnp.einsum('bhs,bhsv,bhsk->bhvk', w_s, v, k)
    n_new = decay[..., None] * n_mem + jnp.einsum('bhs,bhsk->bhk', w_s, k)
    return (c_new, n_new, m_new), h


def _mlstm(q, k, v, i_pre, f_pre):
    bsz, t_all, nh, dh = q.shape
    seq = t_all - N_META
    nc = seq // A_CHUNK
    q = q.transpose(0, 2, 1, 3)
    k = k.transpose(0, 2, 1, 3) * (dh ** -0.5)
    v = v.transpose(0, 2, 1, 3)
    log_i = i_pre.transpose(0, 2, 1)
    log_f = jax.nn.log_sigmoid(f_pre).transpose(0, 2, 1)
    state = (jnp.zeros((bsz, nh, dh, dh), jnp.float32),
             jnp.zeros((bsz, nh, dh), jnp.float32),
             jnp.zeros((bsz, nh), jnp.float32))
    state, h_meta = _mlstm_chunk(
        state, (q[:, :, :N_META], k[:, :, :N_META], v[:, :, :N_META],
                log_i[:, :, :N_META], log_f[:, :, :N_META]))

    def chunks(a):
        a = a[:, :, N_META:]
        return jnp.moveaxis(a.reshape(bsz, nh, nc, A_CHUNK, *a.shape[3:]), 2, 0)

    _, h_rest = lax.scan(_mlstm_chunk, state,
                         (chunks(q), chunks(k), chunks(v), chunks(log_i), chunks(log_f)))
    h_rest = jnp.moveaxis(h_rest, 0, 2).reshape(bsz, nh, seq, dh)
    h = jnp.concatenate([h_meta, h_rest], axis=2)
    return h.transpose(0, 2, 1, 3)


def _rg_lru_branch(xb, gate, conv_w, conv_b, w_r, b_r, w_i, b_i, lam):
    bsz, t_all, w = xb.shape
    xc = lax.conv_general_dilated(
        xb.astype(jnp.float32), conv_w.astype(jnp.float32)[:, None, :],
        window_strides=(1,), padding=[(B_CONV - 1, 0)],
        dimension_numbers=('NWC', 'WIO', 'NWC'), feature_group_count=w)
    xc = xc + conv_b.astype(jnp.float32)
    xg = xc.reshape(bsz, t_all, B_BLOCKS, B_BLOCK_DIM)
    r = jax.nn.sigmoid(jnp.einsum('btnd,nde->btne', xg, w_r.astype(jnp.float32)).reshape(bsz, t_all, w)
                       + b_r.astype(jnp.float32))
    i = jax.nn.sigmoid(jnp.einsum('btnd,nde->btne', xg, w_i.astype(jnp.float32)).reshape(bsz, t_all, w)
                       + b_i.astype(jnp.float32))
    log_a = -LRU_C * r * jax.nn.softplus(-lam.astype(jnp.float32))
    a = jnp.exp(log_a)
    u = jnp.sqrt(-jnp.expm1(2.0 * log_a)) * (i * xc)

    def combine(left, right):
        a1, b1 = left
        a2, b2 = right
        return a1 * a2, a2 * b1 + b2

    _, h = lax.associative_scan(combine, (a, u), axis=1)
    return h * jax.nn.gelu(gate.astype(jnp.float32))


def _ab_mixer(hn, w_in, if_bias, mlstm_norm, conv_w, conv_b, w_r, b_r, w_i, b_i, lam, w_out):
    bsz, t_all, _ = hn.shape
    proj = hn @ w_in
    q, k, v, o, gi, gf, xb, gate = _split(proj, AB_SPLITS)
    f32 = jnp.float32
    hs = (bsz, t_all, A_HEADS, A_HEAD_DIM)
    ifb = if_bias.astype(f32)
    h_a = _mlstm(q.astype(f32).reshape(hs), k.astype(f32).reshape(hs), v.astype(f32).reshape(hs),
                 gi.astype(f32) + ifb[:A_HEADS], gf.astype(f32) + ifb[A_HEADS:])
    h_a = jax.nn.sigmoid(o.astype(f32)).reshape(hs) * h_a
    h_a = h_a * lax.rsqrt(jnp.mean(h_a * h_a, axis=-1, keepdims=True) + EPS)
    h_a = (h_a * mlstm_norm.astype(f32).reshape(A_HEADS, A_HEAD_DIM)).reshape(bsz, t_all, A_WIDTH)
    h_b = _rg_lru_branch(xb, gate, conv_w, conv_b, w_r, b_r, w_i, b_i, lam)
    y = jnp.concatenate([h_a, h_b], axis=-1).astype(hn.dtype)
    return y @ w_out


def _partial_rope(x, cos, sin):
    half = ROT_DIM // 2
    x1, x2, rest = x[..., :half], x[..., half:ROT_DIM], x[..., ROT_DIM:]
    c = cos[:, None, None, :]
    s = sin[:, None, None, :]
    return jnp.concatenate([x1 * c - x2 * s, x2 * c + x1 * s, rest], axis=-1)


def _diff_attend(qb, q_pos, kh, vh, k_pos, lam):
    s = jnp.einsum('bhcqd,bhckd->bhcqk', qb, kh)
    mask = k_pos[None, :] <= q_pos[:, None]
    p = jax.nn.softmax(jnp.where(mask, s, -jnp.inf), axis=-1)
    pd = p[:, :, 0] - lam * p[:, :, 1]
    return jnp.einsum('bhqk,bhkv->bhqv', pd, vh)


def _diff_attn(hn, w_in, lam_vecs, subln, w_out, lambda_init):
    bsz, t_all, _ = hn.shape
    seq = t_all - N_META
    f32 = jnp.float32
    q, k, v = _split(hn @ w_in, C_SPLITS)
    q = q.astype(f32).reshape(bsz, t_all, C_HEADS, 2, C_QK_DIM)
    k = k.astype(f32).reshape(bsz, t_all, C_HEADS, 2, C_QK_DIM)
    v = v.astype(f32).reshape(bsz, t_all, C_HEADS, C_V_DIM)
    pos = jnp.arange(t_all, dtype=jnp.int32)
    inv_freq = jnp.power(jnp.float32(ROPE_THETA), -jnp.arange(0, ROT_DIM, 2, dtype=f32) / ROT_DIM)
    ang = pos.astype(f32)[:, None] * inv_freq[None, :]
    cos, sin = jnp.cos(ang), jnp.sin(ang)
    q = _partial_rope(q, cos, sin) * (C_QK_DIM ** -0.5)
    k = _partial_rope(k, cos, sin)
    lv = lam_vecs.astype(f32)
    lam = jnp.exp(jnp.sum(lv[0] * lv[1])) - jnp.exp(jnp.sum(lv[2] * lv[3])) + lambda_init
    qh = q.transpose(0, 2, 3, 1, 4)
    kh = k.transpose(0, 2, 3, 1, 4)
    vh = v.transpose(0, 2, 1, 3)
    o_meta = _diff_attend(qh[:, :, :, :N_META], pos[:N_META], kh[:, :, :, :N_META],
                          vh[:, :, :N_META], pos[:N_META], lam)
    nb = seq // Q_BLOCK
    q_blocks = jnp.moveaxis(qh[:, :, :, N_META:].reshape(bsz, C_HEADS, 2, nb, Q_BLOCK, C_QK_DIM), 3, 0)
    pos_blocks = pos[N_META:].reshape(nb, Q_BLOCK)
    o_rest = lax.map(lambda a: _diff_attend(a[0], a[1], kh, vh, pos, lam), (q_blocks, pos_blocks))
    o_rest = jnp.moveaxis(o_rest, 0, 2).reshape(bsz, C_HEADS, seq, C_V_DIM)
    o = jnp.concatenate([o_meta, o_rest], axis=2)
    o = o * lax.rsqrt(jnp.mean(o * o, axis=-1, keepdims=True) + EPS) * subln.astype(f32)
    o = o * (1.0 - lambda_init)
    o = o.transpose(0, 2, 1, 3).reshape(bsz, t_all, C_HEADS * C_V_DIM).astype(hn.dtype)
    return o @ w_out


def _sq_relu_mlp(hn, w1, w2):
    a = jax.nn.relu(hn @ w1)
    return (a * a) @ w2


def setup_inputs(seed: int = 0) -> dict:
    key = jax.random.key(seed)
    ks = jax.random.split(key, 24)
    f32 = jnp.float32

    def nrm(k, shape, scale):
        return jax.random.normal(k, shape, f32) * scale

    x = nrm(ks[0], (BATCH, SEQ, D_MODEL), 1.0)
    meta_tokens = nrm(ks[1], (N_META, D_MODEL), 1.0)
    norm_mix = 1.0 + nrm(ks[2], (DEPTH, D_MODEL), 0.05)
    norm_mlp = 1.0 + nrm(ks[3], (DEPTH, D_MODEL), 0.05)
    norm_final = 1.0 + nrm(ks[4], (D_MODEL,), 0.05)
    ab_w_in = nrm(ks[5], (N_AB, D_MODEL, AB_IN), D_MODEL ** -0.5)
    i_b = -1.0 + nrm(ks[6], (N_AB, A_HEADS), 0.1)
    f_b = jnp.linspace(3.0, 6.0, A_HEADS, dtype=f32)[None, :] + nrm(ks[7], (N_AB, A_HEADS), 0.1)
    ab_if_bias = jnp.concatenate([i_b, f_b], axis=-1)
    mlstm_norm = 1.0 + nrm(ks[8], (N_AB, A_WIDTH), 0.05)
    lru_conv_w = nrm(ks[9], (N_AB, B_CONV, B_WIDTH), B_CONV ** -0.5)
    lru_conv_b = nrm(ks[10], (N_AB, B_WIDTH), 0.01)
    lru_w_r = nrm(ks[11], (N_AB, B_BLOCKS, B_BLOCK_DIM, B_BLOCK_DIM), B_BLOCK_DIM ** -0.5)
    lru_b_r = nrm(ks[12], (N_AB, B_WIDTH), 0.01)
    lru_w_i = nrm(ks[13], (N_AB, B_BLOCKS, B_BLOCK_DIM, B_BLOCK_DIM), B_BLOCK_DIM ** -0.5)
    lru_b_i = nrm(ks[14], (N_AB, B_WIDTH), 0.01)
    u = jax.random.uniform(ks[15], (N_AB, B_WIDTH), f32, 0.9, 0.999)
    p = u ** (1.0 / LRU_C)
    lru_lambda = jnp.log(p) - jnp.log1p(-p)
    ab_w_out = nrm(ks[16], (N_AB, A_WIDTH + B_WIDTH, D_MODEL), (A_WIDTH + B_WIDTH) ** -0.5)
    c_w_in = nrm(ks[17], (N_C, D_MODEL, C_IN), D_MODEL ** -0.5)
    c_lambda = nrm(ks[18], (N_C, 4, C_QK_DIM), 0.1)
    c_subln = 1.0 + nrm(ks[19], (N_C, C_V_DIM), 0.05)
    c_w_out = nrm(ks[20], (N_C, C_HEADS * C_V_DIM, D_MODEL), (C_HEADS * C_V_DIM) ** -0.5)
    mlp_w1 = nrm(ks[21], (DEPTH, D_MODEL, D_FF), D_MODEL ** -0.5)
    mlp_w2 = nrm(ks[22], (DEPTH, D_FF, D_MODEL), D_FF ** -0.5)
    return {"x": x, "meta_tokens": meta_tokens, "norm_mix": norm_mix, "norm_mlp": norm_mlp,
            "norm_final": norm_final, "ab_w_in": ab_w_in, "ab_if_bias": ab_if_bias,
            "mlstm_norm": mlstm_norm, "lru_conv_w": lru_conv_w, "lru_conv_b": lru_conv_b,
            "lru_w_r": lru_w_r, "lru_b_r": lru_b_r, "lru_w_i": lru_w_i, "lru_b_i": lru_b_i,
            "lru_lambda": lru_lambda, "ab_w_out": ab_w_out, "c_w_in": c_w_in,
            "c_lambda": c_lambda, "c_subln": c_subln, "c_w_out": c_w_out,
            "mlp_w1": mlp_w1, "mlp_w2": mlp_w2}


def reference(x, meta_tokens, norm_mix, norm_mlp, norm_final, ab_w_in, ab_if_bias,
              mlstm_norm, lru_conv_w, lru_conv_b, lru_w_r, lru_b_r, lru_w_i, lru_b_i,
              lru_lambda, ab_w_out, c_w_in, c_lambda, c_subln, c_w_out, mlp_w1, mlp_w2):
    bsz = x.shape[0]
    meta = jnp.broadcast_to(meta_tokens[None].astype(x.dtype), (bsz, N_META, x.shape[-1]))
    h = jnp.concatenate([meta, x], axis=1)
    for layer in range(DEPTH):
        j = layer // 2
        hn = _rmsnorm(h, norm_mix[layer])
        if layer % 2 == 0:
            h = h + _ab_mixer(hn, ab_w_in[j], ab_if_bias[j], mlstm_norm[j], lru_conv_w[j],
                              lru_conv_b[j], lru_w_r[j], lru_b_r[j], lru_w_i[j], lru_b_i[j],
                              lru_lambda[j], ab_w_out[j])
        else:
            lambda_init = 0.8 - 0.6 * math.exp(-0.3 * layer)
            h = h + _diff_attn(hn, c_w_in[j], c_lambda[j], c_subln[j], c_w_out[j], lambda_init)
        h = h + _sq_relu_mlp(_rmsnorm(h, norm_mlp[layer]), mlp_w1[layer], mlp_w2[layer])
    h = _rmsnorm(h, norm_final)
    return h[:, N_META:, :]
```

```python
import functools
import math

import jax
import jax.numpy as jnp
from jax import lax
from jax.experimental import pallas as pl
from jax.experimental.pallas import tpu as pltpu

F32 = jnp.float32
BF16 = jnp.bfloat16

N_META = 16
EPS = 1e-6
A_HEADS = 4
B_BLOCKS = 8
B_CONV = 4
LRU_C = 8.0
ROPE_THETA = 500000.0

LANES = 128
SUBLANES = 8
GATE_ROWS = 16
NEG = -1e30
VMEM_LIMIT_BYTES = 56 * 1024 * 1024

_NT = (((1,), (1,)), ((), ()))
_TN = (((0,), (0,)), ((), ()))


def _cparams(*sem):
    return pltpu.CompilerParams(dimension_semantics=sem, vmem_limit_bytes=VMEM_LIMIT_BYTES)


def _sigmoid(x):
    return 1.0 / (1.0 + jnp.exp(-x))


def _log_sigmoid(x):
    return jnp.minimum(x, 0.0) - jnp.log1p(jnp.exp(-jnp.abs(x)))


def _softplus(x):
    return jnp.maximum(x, 0.0) + jnp.log1p(jnp.exp(-jnp.abs(x)))


def _rms_scale(x):
    return x * lax.rsqrt(jnp.mean(x * x, axis=-1, keepdims=True) + EPS)


def _pick_tile(n, pref):
    t = min(n, pref)
    while n % t:
        t //= 2
    return t


def _norm_proj_body(*refs, has_rope, has_gates, tn):
    it = iter(refs)
    x_ref, g_ref, w_ref = next(it), next(it), next(it)
    if has_rope:
        cos_ref, sa_ref, sb_ref = next(it), next(it), next(it)
    if has_gates:
        wgc_ref, wgr_ref, bc_ref, br_ref = next(it), next(it), next(it), next(it)
    o_ref = next(it)
    if has_gates:
        gcol_ref, grow_ref = next(it), next(it)
    xn_ref = next(it)

    @pl.when(pl.program_id(1) == 0)
    def _():
        xn = (_rms_scale(x_ref[...]) * g_ref[...]).astype(BF16)
        xn_ref[...] = xn
        if has_gates:
            gc = jnp.dot(xn, wgc_ref[...], preferred_element_type=F32) + bc_ref[...]
            lane = lax.broadcasted_iota(jnp.int32, gc.shape, 1)
            gcol_ref[...] = jnp.where(lane < A_HEADS, gc, _log_sigmoid(gc))
            gr = lax.dot_general(wgr_ref[...], xn, _NT, preferred_element_type=F32) + br_ref[...]
            sub = lax.broadcasted_iota(jnp.int32, gr.shape, 0)
            grow_ref[...] = jnp.where(sub < A_HEADS, gr, _log_sigmoid(gr))

    acc = jnp.dot(xn_ref[...], w_ref[...], preferred_element_type=F32)
    if has_rope:
        cos, sa, sb = cos_ref[...], sa_ref[...], sb_ref[...]
        for c in range(tn // LANES):
            y = acc[:, c * LANES:(c + 1) * LANES]
            y = y * cos + pltpu.roll(y, LANES - 8, 1) * sa + pltpu.roll(y, 8, 1) * sb
            o_ref[:, c * LANES:(c + 1) * LANES] = y.astype(o_ref.dtype)
    else:
        o_ref[...] = acc.astype(o_ref.dtype)


def _norm_proj(x, g, w, *, tm, tn, rope=None, gates=None):
    n, d = x.shape
    nout = w.shape[1]
    grid = (n // tm, nout // tn)
    in_specs = [pl.BlockSpec((tm, d), lambda i, j: (i, 0)),
                pl.BlockSpec((1, d), lambda i, j: (0, 0)),
                pl.BlockSpec((d, tn), lambda i, j: (0, j))]
    args = [x, g.reshape(1, d), w]
    if rope is not None:
        n_sec, period, _ = rope[0].shape
        sec_tiles = nout // n_sec // tn
        p_tiles = period // tm
        spec = pl.BlockSpec((None, tm, LANES), lambda i, j: (j // sec_tiles, i % p_tiles, 0))
        in_specs += [spec, spec, spec]
        args += list(rope)
    out_shape = [jax.ShapeDtypeStruct((n, nout), BF16)]
    out_specs = [pl.BlockSpec((tm, tn), lambda i, j: (i, j))]
    if gates is not None:
        in_specs += [pl.BlockSpec((d, LANES), lambda i, j: (0, 0)),
                     pl.BlockSpec((GATE_ROWS, d), lambda i, j: (0, 0)),
                     pl.BlockSpec((1, LANES), lambda i, j: (0, 0)),
                     pl.BlockSpec((GATE_ROWS, 1), lambda i, j: (0, 0))]
        args += list(gates)
        out_shape += [jax.ShapeDtypeStruct((n, LANES), F32), jax.ShapeDtypeStruct((GATE_ROWS, n), F32)]
        out_specs += [pl.BlockSpec((tm, LANES), lambda i, j: (i, 0)),
                      pl.BlockSpec((GATE_ROWS, tm), lambda i, j: (0, i))]
    body = functools.partial(_norm_proj_body, has_rope=rope is not None, has_gates=gates is not None, tn=tn)
    outs = pl.pallas_call(
        body, grid=grid, in_specs=in_specs, out_specs=out_specs, out_shape=out_shape,
        scratch_shapes=[pltpu.VMEM((tm, d), BF16)],
        compiler_params=_cparams("parallel", "arbitrary"), name="norm_proj")(*args)
    return outs if gates is not None else outs[0]


def _mlstm_body(q_ref, k_ref, v_ref, o_ref, gcol_ref, grow_ref, nw_ref, c0_ref, n0_ref, m0_ref,
                y_ref, c_out, n_out, m_out, c_s, n_s, m_s, *, chunk, dh):
    ci = pl.program_id(1)

    @pl.when(ci == 0)
    def _():
        c_s[...] = c0_ref[...]
        n_s[...] = n0_ref[...]
        m_s[...] = m0_ref[...]

    row = lax.broadcasted_iota(jnp.int32, (chunk, chunk), 0)
    col = lax.broadcasted_iota(jnp.int32, (chunk, chunk), 1)
    causal = col <= row
    bcol = jnp.dot(causal.astype(F32), gcol_ref[...], precision=lax.Precision.HIGHEST,
                   preferred_element_type=F32)
    brow = jnp.dot(grow_ref[...], (row <= col).astype(F32), precision=lax.Precision.HIGHEST,
                   preferred_element_type=F32)
    scale = dh ** -0.5
    for h in range(A_HEADS):
        sl = slice(h * dh, (h + 1) * dh)
        q = q_ref[:, sl]
        k = k_ref[:, sl] * scale
        v = v_ref[:, sl]
        li_c = gcol_ref[:, h:h + 1]
        b_c = bcol[:, A_HEADS + h:A_HEADS + h + 1]
        li_r = grow_ref[h:h + 1, :]
        b_r = brow[A_HEADS + h:A_HEADS + h + 1, :]
        b_end = b_c[chunk - 1:chunk, :]
        m_prev = m_s[h][:, 0:1]
        c_t = c_s[h]
        n_row = n_s[h]

        logd = jnp.where(causal, b_c - b_r + li_r, NEG)
        log_prev = b_c + m_prev
        m_t = jnp.maximum(log_prev, jnp.max(logd, axis=1, keepdims=True))
        d = jnp.exp(logd - m_t)
        w_prev = jnp.exp(log_prev - m_t)
        s = lax.dot_general(q, k, _NT, preferred_element_type=F32) * d
        num = (w_prev * jnp.dot(q, c_t.astype(BF16), preferred_element_type=F32)
               + jnp.dot(s.astype(BF16), v, preferred_element_type=F32))
        qn = jnp.sum(q.astype(F32) * n_row, axis=1, keepdims=True)
        den = w_prev * qn + jnp.sum(s, axis=1, keepdims=True)
        hh = num / jnp.maximum(jnp.abs(den), jnp.exp(-m_t))
        ha = _sigmoid(o_ref[:, sl].astype(F32)) * hh
        y_ref[:, sl] = (_rms_scale(ha) * nw_ref[:, sl]).astype(y_ref.dtype)

        m_new = jnp.maximum(b_end + m_prev, jnp.max(b_end - b_r + li_r, axis=1, keepdims=True))
        w_c = jnp.exp(b_end - b_c + li_c - m_new)
        decay = jnp.exp(b_end + m_prev - m_new)
        kw = k.astype(F32) * w_c
        c_s[h] = decay * c_t + lax.dot_general(kw.astype(BF16), v, _TN, preferred_element_type=F32)
        n_s[h] = decay * n_row + jnp.sum(kw, axis=0, keepdims=True)
        m_s[h] = jnp.broadcast_to(m_new, (1, LANES))

    @pl.when(ci == pl.num_programs(1) - 1)
    def _():
        c_out[...] = c_s[...]
        n_out[...] = n_s[...]
        m_out[...] = m_s[...]


def _mlstm(proj, gcol, grow, norm_w, state, *, nb, chunk, dh):
    n = gcol.shape[0]
    nc = n // nb // chunk
    width = A_HEADS * dh
    c0, n0, m0 = state

    def tok(cb):
        return pl.BlockSpec((chunk, width), lambda b, c: (b * nc + c, cb))

    def whole(a):
        return pl.BlockSpec(a.shape, lambda b, c: (0,) * a.ndim)

    def per_seq(a):
        return pl.BlockSpec((None,) + a.shape, lambda b, c: (b,) + (0,) * a.ndim)

    nw = norm_w.reshape(1, width)
    in_specs = [tok(0), tok(1), tok(2), tok(3),
                pl.BlockSpec((chunk, LANES), lambda b, c: (b * nc + c, 0)),
                pl.BlockSpec((GATE_ROWS, chunk), lambda b, c: (0, b * nc + c)),
                whole(nw), whole(c0), whole(n0), whole(m0)]
    out_shape = [jax.ShapeDtypeStruct((n, width), BF16)] + [
        jax.ShapeDtypeStruct((nb,) + a.shape, F32) for a in state]
    out_specs = [pl.BlockSpec((chunk, width), lambda b, c: (b * nc + c, 0))] + [per_seq(a) for a in state]
    return pl.pallas_call(
        functools.partial(_mlstm_body, chunk=chunk, dh=dh),
        grid=(nb, nc), in_specs=in_specs, out_specs=out_specs, out_shape=out_shape,
        scratch_shapes=[pltpu.VMEM(a.shape, F32) for a in state],
        compiler_params=_cparams("parallel", "arbitrary"), name="mlstm",
    )(proj, proj, proj, proj, gcol, grow, nw, c0, n0, m0)


def _rglru_body(xb_ref, gate_ref, cw_ref, cb_ref, wri_ref, bri_ref, lam_ref, h0_ref, tail0_ref,
                y_ref, h_out, tail_out, h_s, tail_s, *, tt, bd):
    @pl.when(pl.program_id(1) == 0)
    def _():
        h_s[...] = h0_ref[...]
        tail_s[...] = tail0_ref[...]

    row = lax.broadcasted_iota(jnp.int32, (tt, bd), 0)
    row8 = row[:SUBLANES]
    for n in range(B_BLOCKS):
        sl = slice(n * bd, (n + 1) * bd)
        x = xb_ref[:, sl].astype(F32)
        prev_tail = tail_s[:, sl]
        xc = x * cw_ref[B_CONV - 1:B_CONV, sl] + cb_ref[:, sl]
        for s in range(1, B_CONV):
            xs = pltpu.roll(x, s, 0)
            head = jnp.where(row8 < s, pltpu.roll(prev_tail, s, 0), xs[:SUBLANES])
            xs = jnp.concatenate([head, xs[SUBLANES:]], axis=0)
            xc = xc + xs * cw_ref[B_CONV - 1 - s:B_CONV - s, sl]
        tail_s[:, sl] = x[tt - SUBLANES:, :]

        ri = jnp.dot(xc.astype(BF16), wri_ref[n], preferred_element_type=F32) + bri_ref[n]
        r = _sigmoid(ri[:, :bd])
        i = _sigmoid(ri[:, bd:])
        log_a = (-LRU_C * _softplus(-lam_ref[:, sl])) * r
        a = jnp.exp(log_a)
        u = jnp.sqrt(1.0 - a * a) * (i * xc)
        sh = 1
        while sh < tt:
            keep = row >= sh
            u = jnp.where(keep, a * pltpu.roll(u, sh, 0) + u, u)
            a = jnp.where(keep, a * pltpu.roll(a, sh, 0), a)
            sh *= 2
        hh = a * h_s[:, sl] + u
        h_s[:, sl] = hh[tt - 1:tt, :]
        y_ref[:, sl] = (hh * jax.nn.gelu(gate_ref[:, sl].astype(F32))).astype(y_ref.dtype)

    @pl.when(pl.program_id(1) == pl.num_programs(1) - 1)
    def _():
        h_out[...] = h_s[...]
        tail_out[...] = tail_s[...]


def _rglru(proj, xb_block, gate_block, conv_w, conv_b, w_ri, b_ri, lam, state, *, nb, tt):
    n = proj.shape[0]
    nt = n // nb // tt
    c = conv_w.shape[1]
    bd = c // B_BLOCKS
    h0, tail0 = state

    def whole(a):
        return pl.BlockSpec(a.shape, lambda b, t: (0,) * a.ndim)

    def per_seq(a):
        return pl.BlockSpec((None,) + a.shape, lambda b, t: (b,) + (0,) * a.ndim)

    consts = [conv_w, conv_b.reshape(1, c), w_ri, b_ri, lam.reshape(1, c), h0, tail0]
    in_specs = [pl.BlockSpec((tt, c), lambda b, t: (b * nt + t, xb_block)),
                pl.BlockSpec((tt, c), lambda b, t: (b * nt + t, gate_block))] + [whole(a) for a in consts]
    out_shape = [jax.ShapeDtypeStruct((n, c), BF16)] + [jax.ShapeDtypeStruct((nb,) + a.shape, F32) for a in state]
    out_specs = [pl.BlockSpec((tt, c), lambda b, t: (b * nt + t, 0))] + [per_seq(a) for a in state]
    return pl.pallas_call(
        functools.partial(_rglru_body, tt=tt, bd=bd),
        grid=(nb, nt), in_specs=in_specs, out_specs=out_specs, out_shape=out_shape,
        scratch_shapes=[pltpu.VMEM(a.shape, F32) for a in state],
        compiler_params=_cparams("parallel", "arbitrary"), name="rglru",
    )(proj, proj, *consts)


def _mix_mlp_body(*refs, n_y, final_norm):
    it = iter(refs)
    x_ref = next(it)
    y_refs = [next(it) for _ in range(n_y)]
    wo_refs = [next(it) for _ in range(n_y)]
    g_ref, w1_ref, w2_ref = next(it), next(it), next(it)
    gf_ref = next(it) if final_norm else None
    o_ref, hn_s, acc_s = next(it), next(it), next(it)
    j = pl.program_id(1)

    @pl.when(j == 0)
    def _():
        h = x_ref[...]
        for y_ref, wo_ref in zip(y_refs, wo_refs):
            h = h + jnp.dot(y_ref[...], wo_ref[...], preferred_element_type=F32)
        acc_s[...] = h
        hn_s[...] = (_rms_scale(h) * g_ref[...]).astype(BF16)

    a = jnp.maximum(jnp.dot(hn_s[...], w1_ref[...], preferred_element_type=F32), 0.0)
    acc_s[...] += jnp.dot((a * a).astype(BF16), w2_ref[...], preferred_element_type=F32)

    @pl.when(j == pl.num_programs(1) - 1)
    def _():
        out = acc_s[...]
        if final_norm:
            out = _rms_scale(out) * gf_ref[...]
        o_ref[...] = out


def _mix_mlp(x, ys, wos, g, w1, w2, g_final=None, *, tm, tf):
    n, d = x.shape
    dff = w1.shape[1]
    in_specs = [pl.BlockSpec((tm, d), lambda i, j: (i, 0))]
    in_specs += [pl.BlockSpec((tm, y.shape[1]), lambda i, j: (i, 0)) for y in ys]
    in_specs += [pl.BlockSpec(wo.shape, lambda i, j: (0, 0)) for wo in wos]
    in_specs += [pl.BlockSpec((1, d), lambda i, j: (0, 0)),
                 pl.BlockSpec((d, tf), lambda i, j: (0, j)),
                 pl.BlockSpec((tf, d), lambda i, j: (j, 0))]
    args = [x, *ys, *wos, g.reshape(1, d), w1, w2]
    if g_final is not None:
        in_specs.append(pl.BlockSpec((1, d), lambda i, j: (0, 0)))
        args.append(g_final.reshape(1, d))
    return pl.pallas_call(
        functools.partial(_mix_mlp_body, n_y=len(ys), final_norm=g_final is not None),
        grid=(n // tm, dff // tf), in_specs=in_specs,
        out_specs=pl.BlockSpec((tm, d), lambda i, j: (i, 0)),
        out_shape=jax.ShapeDtypeStruct((n, d), F32),
        scratch_shapes=[pltpu.VMEM((tm, d), BF16), pltpu.VMEM((tm, d), F32)],
        compiler_params=_cparams("parallel", "arbitrary"), name="mix_mlp")(*args)


def _diff_attn_body(q_ref, k_ref, v_ref, km_ref, vm_ref, lv_ref, sub_ref, o_ref, *, tq, dqk, lambda_init):
    qi = pl.program_id(2)
    q = q_ref[...]
    lane = lax.broadcasted_iota(jnp.int32, q.shape, 1)
    zero = jnp.zeros_like(q)
    qs = jnp.concatenate([jnp.where(lane < dqk, q, zero), jnp.where(lane >= dqk, q, zero)], axis=0)

    def scores(k):
        return lax.dot_general(qs, k, _NT, preferred_element_type=F32)

    def online(carry, s, v):
        m, l, acc = carry
        m_new = jnp.maximum(m, jnp.max(s, axis=1, keepdims=True))
        alpha = jnp.exp(m - m_new)
        p = jnp.exp(s - m_new)
        l = alpha * l + jnp.sum(p, axis=1, keepdims=True)
        acc = alpha * acc + jnp.dot(p.astype(BF16), v, preferred_element_type=F32)
        return m_new, l, acc

    s0 = scores(km_ref[...])
    m0 = jnp.max(s0, axis=1, keepdims=True)
    p0 = jnp.exp(s0 - m0)
    carry = (m0, jnp.sum(p0, axis=1, keepdims=True),
             jnp.dot(p0.astype(BF16), vm_ref[...], preferred_element_type=F32))

    def body(j, carry):
        start = pl.multiple_of(j * tq, tq)
        return online(carry, scores(k_ref[pl.ds(start, tq), :]), v_ref[pl.ds(start, tq), :])

    carry = lax.fori_loop(0, qi, body, carry)
    start = pl.multiple_of(qi * tq, tq)
    row = lax.broadcasted_iota(jnp.int32, (2 * tq, tq), 0)
    col = lax.broadcasted_iota(jnp.int32, (2 * tq, tq), 1)
    visible = col <= jnp.where(row >= tq, row - tq, row)
    s_diag = jnp.where(visible, scores(k_ref[pl.ds(start, tq), :]), NEG)
    _, l, acc = online(carry, s_diag, v_ref[pl.ds(start, tq), :])

    o = acc / l
    lv = lv_ref[...]
    lam = (jnp.exp(jnp.sum(lv[0:1] * lv[1:2], axis=1, keepdims=True))
           - jnp.exp(jnp.sum(lv[2:3] * lv[3:4], axis=1, keepdims=True)) + lambda_init)
    od = o[:tq] - lam * o[tq:]
    od = _rms_scale(od) * sub_ref[...] * (1.0 - lambda_init)
    o_ref[...] = od.astype(o_ref.dtype)


def _diff_attn(qkv, k_meta, v_meta, lam_vecs, subln, *, nb, seq, heads, tq, lambda_init):
    n = qkv.shape[0]
    dv = subln.shape[0]
    nq = seq // tq
    body = functools.partial(_diff_attn_body, tq=tq, dqk=dv // 2, lambda_init=lambda_init)
    return pl.pallas_call(
        body, grid=(nb, heads, nq),
        in_specs=[pl.BlockSpec((tq, dv), lambda b, h, i: (b * nq + i, h)),
                  pl.BlockSpec((seq, dv), lambda b, h, i: (b, heads + h)),
                  pl.BlockSpec((seq, dv), lambda b, h, i: (b, 2 * heads + h)),
                  pl.BlockSpec((None, N_META, dv), lambda b, h, i: (h, 0, 0)),
                  pl.BlockSpec((None, N_META, dv), lambda b, h, i: (h, 0, 0)),
                  pl.BlockSpec(lam_vecs.shape, lambda b, h, i: (0, 0)),
                  pl.BlockSpec((1, dv), lambda b, h, i: (0, 0))],
        out_specs=pl.BlockSpec((tq, dv), lambda b, h, i: (b * nq + i, h)),
        out_shape=jax.ShapeDtypeStruct((n, heads * dv), BF16),
        compiler_params=_cparams("parallel", "parallel", "arbitrary"), name="diff_attn",
    )(qkv, qkv, qkv, k_meta, v_meta, lam_vecs, subln.reshape(1, dv))


def _rope_tables(positions, dqk, rot_dim):
    half = rot_dim // 2
    inv_freq = jnp.power(jnp.float32(ROPE_THETA), -jnp.arange(0, rot_dim, 2, dtype=F32) / rot_dim)
    ang = positions.astype(F32)[:, None] * inv_freq[None, :]
    c = jnp.arange(LANES) % dqk
    idx = c % half
    cos = jnp.where(c < rot_dim, jnp.cos(ang)[:, idx], 1.0)
    sin = jnp.sin(ang)[:, idx]
    sa = jnp.where(c < half, -sin, 0.0)
    sb = jnp.where((c >= half) & (c < rot_dim), sin, 0.0)
    qs = dqk ** -0.5
    one, zero = jnp.ones_like(cos), jnp.zeros_like(cos)
    return (jnp.stack([cos * qs, cos, one]), jnp.stack([sa * qs, sa, zero]), jnp.stack([sb * qs, sb, zero]))


def kernel(x, meta_tokens, norm_mix, norm_mlp, norm_final, ab_w_in, ab_if_bias, mlstm_norm, lru_conv_w,
           lru_conv_b, lru_w_r, lru_b_r, lru_w_i, lru_b_i, lru_lambda, ab_w_out, c_w_in, c_lambda,
           c_subln, c_w_out, mlp_w1, mlp_w2):
    bsz, seq, d = x.shape
    n = bsz * seq
    assert norm_mix.shape[0] == 2 and meta_tokens.shape[0] == N_META
    a_width = mlstm_norm.shape[1]
    dh = a_width // A_HEADS
    b_width = lru_conv_w.shape[2]
    bd = b_width // B_BLOCKS
    assert a_width == b_width == d
    c_heads = d // LANES
    dv = c_subln.shape[1]
    dqk = dv // 2
    rot_dim = dqk // 4

    tm_proj = _pick_tile(n, 512)
    chunk = _pick_tile(seq, 256)
    tt = _pick_tile(seq, 256)
    tm_mlp = _pick_tile(n, 512)
    tf = 512
    tq = _pick_tile(seq, 256)

    xf = x.reshape(n, d)
    meta = meta_tokens.astype(x.dtype)

    w_in = ab_w_in[0]
    n_qkvo = 4 * a_width
    w_main = jnp.concatenate([w_in[:, :n_qkvo], w_in[:, n_qkvo + 2 * A_HEADS:]], axis=1).astype(BF16)
    w_g = w_in[:, n_qkvo:n_qkvo + 2 * A_HEADS]
    wg_col = jnp.pad(w_g, ((0, 0), (0, LANES - 2 * A_HEADS))).astype(BF16)
    wg_row = jnp.pad(w_g.T, ((0, GATE_ROWS - 2 * A_HEADS), (0, 0))).astype(BF16)
    ifb = ab_if_bias[0].astype(F32)
    gates = (wg_col, wg_row, jnp.pad(ifb, (0, LANES - 2 * A_HEADS)).reshape(1, LANES),
             jnp.pad(ifb, (0, GATE_ROWS - 2 * A_HEADS)).reshape(GATE_ROWS, 1))
    w_ri = jnp.concatenate([lru_w_r[0], lru_w_i[0]], axis=-1).astype(BF16)
    b_ri = jnp.concatenate([lru_b_r[0].reshape(B_BLOCKS, 1, bd), lru_b_i[0].reshape(B_BLOCKS, 1, bd)],
                           axis=-1).astype(F32)
    wo_a = ab_w_out[0][:a_width].astype(BF16)
    wo_b = ab_w_out[0][a_width:].astype(BF16)
    w1 = [mlp_w1[l].astype(BF16) for l in range(2)]
    w2 = [mlp_w2[l].astype(BF16) for l in range(2)]

    def layer0(rows, nb, t_rows, tm, chunk_, tt_, state_a, state_b):
        proj, gcol, grow = _norm_proj(rows, norm_mix[0], w_main, tm=tm, tn=512, gates=gates)
        y_a, *fin_a = _mlstm(proj, gcol, grow, mlstm_norm[0], state_a, nb=nb, chunk=chunk_, dh=dh)
        y_b, *fin_b = _rglru(proj, 4, 5, lru_conv_w[0], lru_conv_b[0], w_ri, b_ri, lru_lambda[0], state_b,
                             nb=nb, tt=tt_)
        h = _mix_mlp(rows, [y_a, y_b], [wo_a, wo_b], norm_mlp[0], w1[0], w2[0], tm=tm, tf=tf)
        return h, fin_a, fin_b

    zero_a = (jnp.zeros((A_HEADS, dh, dh), F32), jnp.zeros((A_HEADS, 1, dh), F32),
              jnp.zeros((A_HEADS, 1, LANES), F32))
    zero_b = (jnp.zeros((1, b_width), F32), jnp.zeros((SUBLANES, b_width), F32))
    h_meta, fin_a, fin_b = layer0(meta, 1, N_META, N_META, N_META, N_META, zero_a, zero_b)
    h_real, _, _ = layer0(xf, bsz, seq, tm_proj, chunk, tt, tuple(a[0] for a in fin_a),
                          tuple(a[0] for a in fin_b))

    lambda_init = 0.8 - 0.6 * math.exp(-0.3 * 1)
    w_qkv = c_w_in[0].astype(BF16)
    pos = jnp.arange(N_META + seq, dtype=jnp.int32)
    rope_meta = _rope_tables(pos[:N_META], dqk, rot_dim)
    rope_real = _rope_tables(pos[N_META:], dqk, rot_dim)
    qkv_meta = _norm_proj(h_meta, norm_mix[1], w_qkv, tm=N_META, tn=512, rope=rope_meta)
    k_meta = qkv_meta[:, d:2 * d].reshape(N_META, c_heads, dv).transpose(1, 0, 2)
    v_meta = qkv_meta[:, 2 * d:].reshape(N_META, c_heads, dv).transpose(1, 0, 2)
    qkv = _norm_proj(h_real, norm_mix[1], w_qkv, tm=_pick_tile(seq, 512), tn=512, rope=rope_real)
    o = _diff_attn(qkv, k_meta, v_meta, c_lambda[0].astype(F32), c_subln[0].astype(F32), nb=bsz, seq=seq,
                   heads=c_heads, tq=tq, lambda_init=lambda_init)
    out = _mix_mlp(h_real, [o], [c_w_out[0].astype(BF16)], norm_mlp[1], w1[1], w2[1], norm_final,
                   tm=tm_mlp, tf=tf)
    return out.reshape(bsz, seq, d)
```

```python
import functools
import math

import jax
import jax.numpy as jnp
from jax import lax
from jax.experimental import pallas as pl
from jax.experimental.pallas import tpu as pltpu

F32 = jnp.float32
BF16 = jnp.bfloat16

N_META = 16
EPS = 1e-6
A_HEADS = 4
B_BLOCKS = 8
B_CONV = 4
LRU_C = 8.0
ROPE_THETA = 500000.0

LANES = 128
SUBLANES = 8
GATE_ROWS = 16
NEG = -1e30
VMEM_LIMIT_BYTES = 56 * 1024 * 1024

_NT = (((1,), (1,)), ((), ()))
_TN = (((0,), (0,)), ((), ()))


def _cparams(*sem):
    return pltpu.CompilerParams(dimension_semantics=sem, vmem_limit_bytes=VMEM_LIMIT_BYTES)


def _sigmoid(x):
    return 1.0 / (1.0 + jnp.exp(-x))


def _log_sigmoid(x):
    return jnp.minimum(x, 0.0) - jnp.log1p(jnp.exp(-jnp.abs(x)))


def _softplus(x):
    return jnp.maximum(x, 0.0) + jnp.log1p(jnp.exp(-jnp.abs(x)))


def _rms_scale(x):
    return x * lax.rsqrt(jnp.mean(x * x, axis=-1, keepdims=True) + EPS)


def _pick_tile(n, pref):
    t = min(n, pref)
    while n % t:
        t //= 2
    return t


def _norm_proj_body(*refs, has_rope, has_gates, tn):
    it = iter(refs)
    x_ref, g_ref, w_ref = next(it), next(it), next(it)
    if has_rope:
        cos_ref, sa_ref, sb_ref = next(it), next(it), next(it)
    if has_gates:
        wgc_ref, wgr_ref, bc_ref, br_ref = next(it), next(it), next(it), next(it)
    o_ref = next(it)
    if has_gates:
        gcol_ref, grow_ref = next(it), next(it)
    xn_ref = next(it)

    @pl.when(pl.program_id(1) == 0)
    def _():
        xn = (_rms_scale(x_ref[...]) * g_ref[...]).astype(BF16)
        xn_ref[...] = xn
        if has_gates:
            gc = jnp.dot(xn, wgc_ref[...], preferred_element_type=F32) + bc_ref[...]
            lane = lax.broadcasted_iota(jnp.int32, gc.shape, 1)
            gcol_ref[...] = jnp.where(lane < A_HEADS, gc, _log_sigmoid(gc))
            gr = lax.dot_general(wgr_ref[...], xn, _NT, preferred_element_type=F32) + br_ref[...]
            sub = lax.broadcasted_iota(jnp.int32, gr.shape, 0)
            grow_ref[...] = jnp.where(sub < A_HEADS, gr, _log_sigmoid(gr))

    acc = jnp.dot(xn_ref[...], w_ref[...], preferred_element_type=F32)
    if has_rope:
        cos, sa, sb = cos_ref[...], sa_ref[...], sb_ref[...]
        for c in range(tn // LANES):
            y = acc[:, c * LANES:(c + 1) * LANES]
            y = y * cos + pltpu.roll(y, LANES - 8, 1) * sa + pltpu.roll(y, 8, 1) * sb
            o_ref[:, c * LANES:(c + 1) * LANES] = y.astype(o_ref.dtype)
    else:
        o_ref[...] = acc.astype(o_ref.dtype)


def _norm_proj(x, g, w, *, tm, tn, rope=None, gates=None):
    n, d = x.shape
    nout = w.shape[1]
    grid = (n // tm, nout // tn)
    in_specs = [pl.BlockSpec((tm, d), lambda i, j: (i, 0)),
                pl.BlockSpec((1, d), lambda i, j: (0, 0)),
                pl.BlockSpec((d, tn), lambda i, j: (0, j))]
    args = [x, g.reshape(1, d), w]
    if rope is not None:
        n_sec, period, _ = rope[0].shape
        sec_tiles = nout // n_sec // tn
        p_tiles = period // tm
        spec = pl.BlockSpec((None, tm, LANES), lambda i, j: (j // sec_tiles, i % p_tiles, 0))
        in_specs += [spec, spec, spec]
        args += list(rope)
    out_shape = [jax.ShapeDtypeStruct((n, nout), BF16)]
    out_specs = [pl.BlockSpec((tm, tn), lambda i, j: (i, j))]
    if gates is not None:
        in_specs += [pl.BlockSpec((d, LANES), lambda i, j: (0, 0)),
                     pl.BlockSpec((GATE_ROWS, d), lambda i, j: (0, 0)),
                     pl.BlockSpec((1, LANES), lambda i, j: (0, 0)),
                     pl.BlockSpec((GATE_ROWS, 1), lambda i, j: (0, 0))]
        args += list(gates)
        out_shape += [jax.ShapeDtypeStruct((n, LANES), F32), jax.ShapeDtypeStruct((GATE_ROWS, n), F32)]
        out_specs += [pl.BlockSpec((tm, LANES), lambda i, j: (i, 0)),
                      pl.BlockSpec((GATE_ROWS, tm), lambda i, j: (0, i))]
    body = functools.partial(_norm_proj_body, has_rope=rope is not None, has_gates=gates is not None, tn=tn)
    outs = pl.pallas_call(
        body, grid=grid, in_specs=in_specs, out_specs=out_specs, out_shape=out_shape,
        scratch_shapes=[pltpu.VMEM((tm, d), BF16)],
        compiler_params=_cparams("parallel", "arbitrary"), name="norm_proj")(*args)
    return outs if gates is not None else outs[0]


def _mlstm_body(q_ref, k_ref, v_ref, o_ref, gcol_ref, grow_ref, nw_ref, c0_ref, n0_ref, m0_ref,
                y_ref, c_out, n_out, m_out, c_s, n_s, m_s, *, chunk, dh):
    ci = pl.program_id(1)

    @pl.when(ci == 0)
    def _():
        c_s[...] = c0_ref[...]
        n_s[...] = n0_ref[...]
        m_s[...] = m0_ref[...]

    row = lax.broadcasted_iota(jnp.int32, (chunk, chunk), 0)
    col = lax.broadcasted_iota(jnp.int32, (chunk, chunk), 1)
    causal = col <= row
    bcol = jnp.dot(causal.astype(F32), gcol_ref[...], precision=lax.Precision.HIGHEST,
                   preferred_element_type=F32)
    brow = jnp.dot(grow_ref[...], (row <= col).astype(F32), precision=lax.Precision.HIGHEST,
                   preferred_element_type=F32)
    scale = dh ** -0.5
    for h in range(A_HEADS):
        sl = slice(h * dh, (h + 1) * dh)
        q = q_ref[:, sl]
        k = k_ref[:, sl] * scale
        v = v_ref[:, sl]
        li_c = gcol_ref[:, h:h + 1]
        b_c = bcol[:, A_HEADS + h:A_HEADS + h + 1]
        li_r = grow_ref[h:h + 1, :]
        b_r = brow[A_HEADS + h:A_HEADS + h + 1, :]
        b_end = b_c[chunk - 1:chunk, :]
        m_prev = m_s[h][:, 0:1]
        c_t = c_s[h]
        n_row = n_s[h]

        logd = jnp.where(causal, b_c - b_r + li_r, NEG)
        log_prev = b_c + m_prev
        m_t = jnp.maximum(log_prev, jnp.max(logd, axis=1, keepdims=True))
        d = jnp.exp(logd - m_t)
        w_prev = jnp.exp(log_prev - m_t)
        s = lax.dot_general(q, k, _NT, preferred_element_type=F32) * d
        num = (w_prev * jnp.dot(q, c_t.astype(BF16), preferred_element_type=F32)
               + jnp.dot(s.astype(BF16), v, preferred_element_type=F32))
        qn = jnp.sum(q.astype(F32) * n_row, axis=1, keepdims=True)
        den = w_prev * qn + jnp.sum(s, axis=1, keepdims=True)
        hh = num / jnp.maximum(jnp.abs(den), jnp.exp(-m_t))
        ha = _sigmoid(o_ref[:, sl].astype(F32)) * hh
        y_ref[:, sl] = (_rms_scale(ha) * nw_ref[:, sl]).astype(y_ref.dtype)

        m_new = jnp.maximum(b_end + m_prev, jnp.max(b_end - b_r + li_r, axis=1, keepdims=True))
        w_c = jnp.exp(b_end - b_c + li_c - m_new)
        decay = jnp.exp(b_end + m_prev - m_new)
        kw = k.astype(F32) * w_c
        c_s[h] = decay * c_t + lax.dot_general(kw.astype(BF16), v, _TN, preferred_element_type=F32)
        n_s[h] = decay * n_row + jnp.sum(kw, axis=0, keepdims=True)
        m_s[h] = jnp.broadcast_to(m_new, (1, LANES))

    @pl.when(ci == pl.num_programs(1) - 1)
    def _():
        c_out[...] = c_s[...]
        n_out[...] = n_s[...]
        m_out[...] = m_s[...]


def _mlstm(proj, gcol, grow, norm_w, state, *, nb, chunk, dh):
    n = gcol.shape[0]
    nc = n // nb // chunk
    width = A_HEADS * dh
    c0, n0, m0 = state

    def tok(cb):
        return pl.BlockSpec((chunk, width), lambda b, c: (b * nc + c, cb))

    def whole(a):
        return pl.BlockSpec(a.shape, lambda b, c: (0,) * a.ndim)

    def per_seq(a):
        return pl.BlockSpec((None,) + a.shape, lambda b, c: (b,) + (0,) * a.ndim)

    nw = norm_w.reshape(1, width)
    in_specs = [tok(0), tok(1), tok(2), tok(3),
                pl.BlockSpec((chunk, LANES), lambda b, c: (b * nc + c, 0)),
                pl.BlockSpec((GATE_ROWS, chunk), lambda b, c: (0, b * nc + c)),
                whole(nw), whole(c0), whole(n0), whole(m0)]
    out_shape = [jax.ShapeDtypeStruct((n, width), BF16)] + [
        jax.ShapeDtypeStruct((nb,) + a.shape, F32) for a in state]
    out_specs = [pl.BlockSpec((chunk, width), lambda b, c: (b * nc + c, 0))] + [per_seq(a) for a in state]
    return pl.pallas_call(
        functools.partial(_mlstm_body, chunk=chunk, dh=dh),
        grid=(nb, nc), in_specs=in_specs, out_specs=out_specs, out_shape=out_shape,
        scratch_shapes=[pltpu.VMEM(a.shape, F32) for a in state],
        compiler_params=_cparams("parallel", "arbitrary"), name="mlstm",
    )(proj, proj, proj, proj, gcol, grow, nw, c0, n0, m0)


def _rglru_body(xb_ref, gate_ref, cw_ref, cb_ref, wri_ref, bri_ref, lam_ref, h0_ref, tail0_ref,
                y_ref, h_out, tail_out, h_s, tail_s, *, tt, bd):
    @pl.when(pl.program_id(1) == 0)
    def _():
        h_s[...] = h0_ref[...]
        tail_s[...] = tail0_ref[...]

    row = lax.broadcasted_iota(jnp.int32, (tt, bd), 0)
    row8 = row[:SUBLANES]
    for n in range(B_BLOCKS):
        sl = slice(n * bd, (n + 1) * bd)
        x = xb_ref[:, sl].astype(F32)
        prev_tail = tail_s[:, sl]
        xc = x * cw_ref[B_CONV - 1:B_CONV, sl] + cb_ref[:, sl]
        for s in range(1, B_CONV):
            xs = pltpu.roll(x, s, 0)
            head = jnp.where(row8 < s, pltpu.roll(prev_tail, s, 0), xs[:SUBLANES])
            xs = jnp.concatenate([head, xs[SUBLANES:]], axis=0)
            xc = xc + xs * cw_ref[B_CONV - 1 - s:B_CONV - s, sl]
        tail_s[:, sl] = x[tt - SUBLANES:, :]

        ri = jnp.dot(xc.astype(BF16), wri_ref[n], preferred_element_type=F32) + bri_ref[n]
        r = _sigmoid(ri[:, :bd])
        i = _sigmoid(ri[:, bd:])
        log_a = (-LRU_C * _softplus(-lam_ref[:, sl])) * r
        a = jnp.exp(log_a)
        u = jnp.sqrt(1.0 - a * a) * (i * xc)
        sh = 1
        while sh < tt:
            keep = row >= sh
            u = jnp.where(keep, a * pltpu.roll(u, sh, 0) + u, u)
            a = jnp.where(keep, a * pltpu.roll(a, sh, 0), a)
            sh *= 2
        hh = a * h_s[:, sl] + u
        h_s[:, sl] = hh[tt - 1:tt, :]
        y_ref[:, sl] = (hh * jax.nn.gelu(gate_ref[:, sl].astype(F32))).astype(y_ref.dtype)

    @pl.when(pl.program_id(1) == pl.num_programs(1) - 1)
    def _():
        h_out[...] = h_s[...]
        tail_out[...] = tail_s[...]


def _rglru(proj, xb_block, gate_block, conv_w, conv_b, w_ri, b_ri, lam, state, *, nb, tt):
    n = proj.shape[0]
    nt = n // nb // tt
    c = conv_w.shape[1]
    bd = c // B_BLOCKS
    h0, tail0 = state

    def whole(a):
        return pl.BlockSpec(a.shape, lambda b, t: (0,) * a.ndim)

    def per_seq(a):
        return pl.BlockSpec((None,) + a.shape, lambda b, t: (b,) + (0,) * a.ndim)

    consts = [conv_w, conv_b.reshape(1, c), w_ri, b_ri, lam.reshape(1, c), h0, tail0]
    in_specs = [pl.BlockSpec((tt, c), lambda b, t: (b * nt + t, xb_block)),
                pl.BlockSpec((tt, c), lambda b, t: (b * nt + t, gate_block))] + [whole(a) for a in consts]
    out_shape = [jax.ShapeDtypeStruct((n, c), BF16)] + [jax.ShapeDtypeStruct((nb,) + a.shape, F32) for a in state]
    out_specs = [pl.BlockSpec((tt, c), lambda b, t: (b * nt + t, 0))] + [per_seq(a) for a in state]
    return pl.pallas_call(
        functools.partial(_rglru_body, tt=tt, bd=bd),
        grid=(nb, nt), in_specs=in_specs, out_specs=out_specs, out_shape=out_shape,
        scratch_shapes=[pltpu.VMEM(a.shape, F32) for a in state],
        compiler_params=_cparams("parallel", "arbitrary"), name="rglru",
    )(proj, proj, *consts)


def _mix_mlp_body(*refs, n_y, final_norm):
    it = iter(refs)
    x_ref = next(it)
    y_refs = [next(it) for _ in range(n_y)]
    wo_refs = [next(it) for _ in range(n_y)]
    g_ref, w1_ref, w2_ref = next(it), next(it), next(it)
    gf_ref = next(it) if final_norm else None
    o_ref, hn_s, acc_s = next(it), next(it), next(it)
    j = pl.program_id(1)

    @pl.when(j == 0)
    def _():
        h = x_ref[...]
        for y_ref, wo_ref in zip(y_refs, wo_refs):
            h = h + jnp.dot(y_ref[...], wo_ref[...], preferred_element_type=F32)
        acc_s[...] = h
        hn_s[...] = (_rms_scale(h) * g_ref[...]).astype(BF16)

    a = jnp.maximum(jnp.dot(hn_s[...], w1_ref[...], preferred_element_type=F32), 0.0)
    acc_s[...] += jnp.dot((a * a).astype(BF16), w2_ref[...], preferred_element_type=F32)

    @pl.when(j == pl.num_programs(1) - 1)
    def _():
        out = acc_s[...]
        if final_norm:
            out = _rms_scale(out) * gf_ref[...]
        o_ref[...] = out


def _mix_mlp(x, ys, wos, g, w1, w2, g_final=None, *, tm, tf):
    n, d = x.shape
    dff = w1.shape[1]
    in_specs = [pl.BlockSpec((tm, d), lambda i, j: (i, 0))]
    in_specs += [pl.BlockSpec((tm, y.shape[1]), lambda i, j: (i, 0)) for y in ys]
    in_specs += [pl.BlockSpec(wo.shape, lambda i, j: (0, 0)) for wo in wos]
    in_specs += [pl.BlockSpec((1, d), lambda i, j: (0, 0)),
                 pl.BlockSpec((d, tf), lambda i, j: (0, j)),
                 pl.BlockSpec((tf, d), lambda i, j: (j, 0))]
    args = [x, *ys, *wos, g.reshape(1, d), w1, w2]
    if g_final is not None:
        in_specs.append(pl.BlockSpec((1, d), lambda i, j: (0, 0)))
        args.append(g_final.reshape(1, d))
    return pl.pallas_call(
        functools.partial(_mix_mlp_body, n_y=len(ys), final_norm=g_final is not None),
        grid=(n // tm, dff // tf), in_specs=in_specs,
        out_specs=pl.BlockSpec((tm, d), lambda i, j: (i, 0)),
        out_shape=jax.ShapeDtypeStruct((n, d), F32),
        scratch_shapes=[pltpu.VMEM((tm, d), BF16), pltpu.VMEM((tm, d), F32)],
        compiler_params=_cparams("parallel", "arbitrary"), name="mix_mlp")(*args)


def _diff_attn_body(q_ref, k_ref, v_ref, km_ref, vm_ref, lv_ref, sub_ref, o_ref, s_scr, sm_scr, mb_scr, lp_scr,
                    acc_scr, *, tq, dqk, hp, lambda_init):
    qi = pl.program_id(2)
    dv = 2 * dqk
    n_fold = tq // LANES
    lane_q = lax.broadcasted_iota(jnp.int32, (1, dv), 1)
    sel1 = (lane_q < dqk).astype(BF16)
    sel2 = (lane_q >= dqk).astype(BF16)
    meta_lane = lax.broadcasted_iota(jnp.int32, (2 * tq, LANES), 1) < N_META

    def head(ref, hh, rows=slice(None)):
        return ref[rows, hh * dv:(hh + 1) * dv]

    qs = []
    for hh in range(hp):
        q = head(q_ref, hh)
        qs.append(jnp.concatenate([q * sel1, q * sel2], axis=0))

    def scores(hh, k):
        return lax.dot_general(qs[hh], k, _NT, preferred_element_type=F32)

    def lane_fold(x, op):
        r = x[:, :LANES]
        for c in range(1, n_fold):
            r = op(r, x[:, c * LANES:(c + 1) * LANES])
        return r

    def key_block(j):
        return pl.ds(pl.multiple_of(j * tq, tq), tq)

    for hh in range(hp):
        s_meta = jnp.where(meta_lane, scores(hh, km_ref[hh]), NEG)
        sm_scr[hh] = s_meta
        mb_scr[hh] = s_meta

    def pass1(j, carry):
        for hh in range(hp):
            s = scores(hh, head(k_ref, hh, key_block(j)))
            s_scr[hh, j] = s
            mb_scr[hh] = jnp.maximum(mb_scr[hh], lane_fold(s, jnp.maximum))
        return carry

    lax.fori_loop(0, qi, pass1, 0)
    row = lax.broadcasted_iota(jnp.int32, (2 * tq, tq), 0)
    col = lax.broadcasted_iota(jnp.int32, (2 * tq, tq), 1)
    visible = col <= jnp.where(row >= tq, row - tq, row)
    for hh in range(hp):
        s_diag = jnp.where(visible, scores(hh, head(k_ref, hh, key_block(qi))), NEG)
        s_scr[hh, qi] = s_diag
        m_lanes = jnp.maximum(mb_scr[hh], lane_fold(s_diag, jnp.maximum))
        mb = jnp.broadcast_to(jnp.max(m_lanes, axis=1, keepdims=True), m_lanes.shape)
        mb_scr[hh] = mb
        p_meta = jnp.exp2(sm_scr[hh] - mb)
        lp_scr[hh] = p_meta
        acc_scr[hh] = jnp.dot(p_meta.astype(BF16), vm_ref[hh], preferred_element_type=F32)

    def pass2(j, carry):
        for hh in range(hp):
            mb = mb_scr[hh]
            p = jnp.concatenate(
                [jnp.exp2(s_scr[hh, j, :, c * LANES:(c + 1) * LANES] - mb) for c in range(n_fold)], axis=1)
            lp_scr[hh] += lane_fold(p, jnp.add)
            acc_scr[hh] += jnp.dot(p.astype(BF16), head(v_ref, hh, key_block(j)), preferred_element_type=F32)
        return carry

    lax.fori_loop(0, qi + 1, pass2, 0)
    lv = lv_ref[...]
    lam = (jnp.exp(jnp.sum(lv[0:1] * lv[1:2], axis=1, keepdims=True))
           - jnp.exp(jnp.sum(lv[2:3] * lv[3:4], axis=1, keepdims=True)) + lambda_init)
    for hh in range(hp):
        o = acc_scr[hh] / jnp.sum(lp_scr[hh], axis=1, keepdims=True)
        od = o[:tq] - lam * o[tq:]
        od = _rms_scale(od) * sub_ref[...] * (1.0 - lambda_init)
        o_ref[:, hh * dv:(hh + 1) * dv] = od.astype(o_ref.dtype)


def _diff_attn(qkv, k_meta, v_meta, lam_vecs, subln, *, nb, seq, heads, tq, hp, lambda_init):
    n = qkv.shape[0]
    dv = subln.shape[0]
    nq = seq // tq
    hg = heads // hp
    w = hp * dv
    body = functools.partial(_diff_attn_body, tq=tq, dqk=dv // 2, hp=hp, lambda_init=lambda_init)
    stat = pltpu.VMEM((hp, 2 * tq, LANES), F32)
    return pl.pallas_call(
        body, grid=(nb, hg, nq),
        in_specs=[pl.BlockSpec((tq, w), lambda b, h, i: (b * nq + i, h)),
                  pl.BlockSpec((seq, w), lambda b, h, i: (b, hg + h)),
                  pl.BlockSpec((seq, w), lambda b, h, i: (b, 2 * hg + h)),
                  pl.BlockSpec((hp, LANES, dv), lambda b, h, i: (h, 0, 0)),
                  pl.BlockSpec((hp, LANES, dv), lambda b, h, i: (h, 0, 0)),
                  pl.BlockSpec(lam_vecs.shape, lambda b, h, i: (0, 0)),
                  pl.BlockSpec((1, dv), lambda b, h, i: (0, 0))],
        out_specs=pl.BlockSpec((tq, w), lambda b, h, i: (b * nq + i, h)),
        out_shape=jax.ShapeDtypeStruct((n, heads * dv), BF16),
        scratch_shapes=[pltpu.VMEM((hp, nq, 2 * tq, tq), F32), stat, stat, stat, stat],
        compiler_params=_cparams("parallel", "parallel", "arbitrary"), name="diff_attn",
    )(qkv, qkv, qkv, k_meta, v_meta, lam_vecs, subln.reshape(1, dv))


def _rope_tables(positions, dqk, rot_dim):
    half = rot_dim // 2
    inv_freq = jnp.power(jnp.float32(ROPE_THETA), -jnp.arange(0, rot_dim, 2, dtype=F32) / rot_dim)
    ang = positions.astype(F32)[:, None] * inv_freq[None, :]
    c = jnp.arange(LANES) % dqk
    idx = c % half
    cos = jnp.where(c < rot_dim, jnp.cos(ang)[:, idx], 1.0)
    sin = jnp.sin(ang)[:, idx]
    sa = jnp.where(c < half, -sin, 0.0)
    sb = jnp.where((c >= half) & (c < rot_dim), sin, 0.0)
    qs = dqk ** -0.5 * math.log2(math.e)
    one, zero = jnp.ones_like(cos), jnp.zeros_like(cos)
    return (jnp.stack([cos * qs, cos, one]), jnp.stack([sa * qs, sa, zero]), jnp.stack([sb * qs, sb, zero]))


def kernel(x, meta_tokens, norm_mix, norm_mlp, norm_final, ab_w_in, ab_if_bias, mlstm_norm, lru_conv_w,
           lru_conv_b, lru_w_r, lru_b_r, lru_w_i, lru_b_i, lru_lambda, ab_w_out, c_w_in, c_lambda,
           c_subln, c_w_out, mlp_w1, mlp_w2):
    bsz, seq, d = x.shape
    n = bsz * seq
    assert norm_mix.shape[0] == 2 and meta_tokens.shape[0] == N_META
    a_width = mlstm_norm.shape[1]
    dh = a_width // A_HEADS
    b_width = lru_conv_w.shape[2]
    bd = b_width // B_BLOCKS
    assert a_width == b_width == d
    c_heads = d // LANES
    dv = c_subln.shape[1]
    dqk = dv // 2
    rot_dim = dqk // 4

    tm_proj = _pick_tile(n, 512)
    chunk = _pick_tile(seq, 256)
    tt = _pick_tile(seq, 256)
    tm_mlp = _pick_tile(n, 512)
    tf = 512
    tq = _pick_tile(seq, 512)

    xf = x.reshape(n, d)
    meta = meta_tokens.astype(x.dtype)

    w_in = ab_w_in[0]
    n_qkvo = 4 * a_width
    w_main = jnp.concatenate([w_in[:, :n_qkvo], w_in[:, n_qkvo + 2 * A_HEADS:]], axis=1).astype(BF16)
    w_g = w_in[:, n_qkvo:n_qkvo + 2 * A_HEADS]
    wg_col = jnp.pad(w_g, ((0, 0), (0, LANES - 2 * A_HEADS))).astype(BF16)
    wg_row = jnp.pad(w_g.T, ((0, GATE_ROWS - 2 * A_HEADS), (0, 0))).astype(BF16)
    ifb = ab_if_bias[0].astype(F32)
    gates = (wg_col, wg_row, jnp.pad(ifb, (0, LANES - 2 * A_HEADS)).reshape(1, LANES),
             jnp.pad(ifb, (0, GATE_ROWS - 2 * A_HEADS)).reshape(GATE_ROWS, 1))
    w_ri = jnp.concatenate([lru_w_r[0], lru_w_i[0]], axis=-1).astype(BF16)
    b_ri = jnp.concatenate([lru_b_r[0].reshape(B_BLOCKS, 1, bd), lru_b_i[0].reshape(B_BLOCKS, 1, bd)],
                           axis=-1).astype(F32)
    wo_a = ab_w_out[0][:a_width].astype(BF16)
    wo_b = ab_w_out[0][a_width:].astype(BF16)
    w1 = [mlp_w1[l].astype(BF16) for l in range(2)]
    w2 = [mlp_w2[l].astype(BF16) for l in range(2)]

    def layer0(rows, nb, t_rows, tm, chunk_, tt_, state_a, state_b):
        proj, gcol, grow = _norm_proj(rows, norm_mix[0], w_main, tm=tm, tn=512, gates=gates)
        y_a, *fin_a = _mlstm(proj, gcol, grow, mlstm_norm[0], state_a, nb=nb, chunk=chunk_, dh=dh)
        y_b, *fin_b = _rglru(proj, 4, 5, lru_conv_w[0], lru_conv_b[0], w_ri, b_ri, lru_lambda[0], state_b,
                             nb=nb, tt=tt_)
        h = _mix_mlp(rows, [y_a, y_b], [wo_a, wo_b], norm_mlp[0], w1[0], w2[0], tm=tm, tf=tf)
        return h, fin_a, fin_b

    zero_a = (jnp.zeros((A_HEADS, dh, dh), F32), jnp.zeros((A_HEADS, 1, dh), F32),
              jnp.zeros((A_HEADS, 1, LANES), F32))
    zero_b = (jnp.zeros((1, b_width), F32), jnp.zeros((SUBLANES, b_width), F32))
    h_meta, fin_a, fin_b = layer0(meta, 1, N_META, N_META, N_META, N_META, zero_a, zero_b)
    h_real, _, _ = layer0(xf, bsz, seq, tm_proj, chunk, tt, tuple(a[0] for a in fin_a),
                          tuple(a[0] for a in fin_b))

    lambda_init = 0.8 - 0.6 * math.exp(-0.3 * 1)
    w_qkv = c_w_in[0].astype(BF16)
    pos = jnp.arange(N_META + seq, dtype=jnp.int32)
    rope_meta = _rope_tables(pos[:N_META], dqk, rot_dim)
    rope_real = _rope_tables(pos[N_META:], dqk, rot_dim)
    qkv_meta = _norm_proj(h_meta, norm_mix[1], w_qkv, tm=N_META, tn=512, rope=rope_meta)
    def meta_block(a):
        a = a.reshape(N_META, c_heads, dv).transpose(1, 0, 2)
        return jnp.pad(a, ((0, 0), (0, LANES - N_META), (0, 0)))

    k_meta = meta_block(qkv_meta[:, d:2 * d])
    v_meta = meta_block(qkv_meta[:, 2 * d:])
    qkv = _norm_proj(h_real, norm_mix[1], w_qkv, tm=_pick_tile(seq, 512), tn=512, rope=rope_real)
    o = _diff_attn(qkv, k_meta, v_meta, c_lambda[0].astype(F32), c_subln[0].astype(F32), nb=bsz, seq=seq,
                   heads=c_heads, tq=tq, hp=2, lambda_init=lambda_init)
    out = _mix_mlp(h_real, [o], [c_w_out[0].astype(BF16)], norm_mlp[1], w1[1], w2[1], norm_final,
                   tm=tm_mlp, tf=tf)
    return out.reshape(bsz, seq, d)
```

```python
import functools
import math

import jax
import jax.numpy as jnp
from jax import lax
from jax.experimental import pallas as pl
from jax.experimental.pallas import tpu as pltpu

F32 = jnp.float32
BF16 = jnp.bfloat16

N_META = 16
EPS = 1e-6
A_HEADS = 4
B_BLOCKS = 8
B_CONV = 4
LRU_C = 8.0
ROPE_THETA = 500000.0

LANES = 128
SUBLANES = 8
GATE_ROWS = 16
NEG = -1e30
VMEM_LIMIT_BYTES = 56 * 1024 * 1024

_NT = (((1,), (1,)), ((), ()))
_TN = (((0,), (0,)), ((), ()))


def _cparams(*sem):
    return pltpu.CompilerParams(dimension_semantics=sem, vmem_limit_bytes=VMEM_LIMIT_BYTES)


def _sigmoid(x):
    return 1.0 / (1.0 + jnp.exp(-x))


def _log_sigmoid(x):
    return jnp.minimum(x, 0.0) - jnp.log1p(jnp.exp(-jnp.abs(x)))


def _softplus(x):
    return jnp.maximum(x, 0.0) + jnp.log1p(jnp.exp(-jnp.abs(x)))


def _rms_scale(x):
    return x * lax.rsqrt(jnp.mean(x * x, axis=-1, keepdims=True) + EPS)


def _pick_tile(n, pref):
    t = min(n, pref)
    while n % t:
        t //= 2
    return t


def _resident(shape):
    return pl.BlockSpec(shape, lambda *_: (0,) * len(shape), pipeline_mode=pl.Buffered(1))


def _norm_proj_body(*refs, has_rope, has_gates, tn, rope_cols):
    it = iter(refs)
    x_ref, g_ref, w_ref = next(it), next(it), next(it)
    if has_rope:
        cos_ref, sa_ref, sb_ref = next(it), next(it), next(it)
    if has_gates:
        wgc_ref, wgr_ref, bc_ref, br_ref = next(it), next(it), next(it), next(it)
    o_ref = next(it)
    if has_gates:
        gcol_ref, grow_ref = next(it), next(it)
    xn_ref = next(it)

    xn = (_rms_scale(x_ref[...]) * g_ref[...]).astype(BF16)
    xn_ref[...] = xn
    if has_gates:
        gc = jnp.dot(xn, wgc_ref[...], preferred_element_type=F32) + bc_ref[...]
        lane = lax.broadcasted_iota(jnp.int32, gc.shape, 1)
        gcol_ref[...] = jnp.where(lane < A_HEADS, gc, _log_sigmoid(gc))
        gr = lax.dot_general(wgr_ref[...], xn, _NT, preferred_element_type=F32) + br_ref[...]
        sub = lax.broadcasted_iota(jnp.int32, gr.shape, 0)
        grow_ref[...] = jnp.where(sub < A_HEADS, gr, _log_sigmoid(gr))

    for c in range(o_ref.shape[1] // tn):
        acc = jnp.dot(xn_ref[...], w_ref[:, c * tn:(c + 1) * tn], preferred_element_type=F32)
        sec = (c * tn) // rope_cols if has_rope else None
        if has_rope and sec < cos_ref.shape[0]:
            cos, sa, sb = cos_ref[sec], sa_ref[sec], sb_ref[sec]
            for cc in range(tn // LANES):
                y = acc[:, cc * LANES:(cc + 1) * LANES]
                y = y * cos + pltpu.roll(y, LANES - 8, 1) * sa + pltpu.roll(y, 8, 1) * sb
                o_ref[:, c * tn + cc * LANES:c * tn + (cc + 1) * LANES] = y.astype(o_ref.dtype)
        else:
            o_ref[:, c * tn:(c + 1) * tn] = acc.astype(o_ref.dtype)


def _norm_proj(x, g, w, *, tm, tn, rope=None, rope_cols=None, gates=None):
    n, d = x.shape
    nout = w.shape[1]
    period = rope[0].shape[1] if rope is not None else n
    p_tiles = period // tm
    grid = (p_tiles, n // period)

    def rows(width):
        return pl.BlockSpec((tm, width), lambda p, r: (r * p_tiles + p, 0))

    in_specs = [rows(d), _resident((1, d)), _resident(w.shape)]
    args = [x, g.reshape(1, d), w]
    if rope is not None:
        spec = pl.BlockSpec((rope[0].shape[0], tm, LANES), lambda p, r: (0, p, 0))
        in_specs += [spec, spec, spec]
        args += list(rope)
    out_shape = [jax.ShapeDtypeStruct((n, nout), BF16)]
    out_specs = [rows(nout)]
    if gates is not None:
        in_specs += [_resident(a.shape) for a in gates]
        args += list(gates)
        out_shape += [jax.ShapeDtypeStruct((n, LANES), F32), jax.ShapeDtypeStruct((GATE_ROWS, n), F32)]
        out_specs += [rows(LANES), pl.BlockSpec((GATE_ROWS, tm), lambda p, r: (0, r * p_tiles + p))]
    body = functools.partial(_norm_proj_body, has_rope=rope is not None, has_gates=gates is not None, tn=tn,
                             rope_cols=rope_cols)
    outs = pl.pallas_call(
        body, grid=grid, in_specs=in_specs, out_specs=out_specs, out_shape=out_shape,
        scratch_shapes=[pltpu.VMEM((tm, d), BF16)],
        compiler_params=_cparams("parallel", "parallel"), name="norm_proj")(*args)
    return outs if gates is not None else outs[0]


def _mlstm_body(q_ref, k_ref, v_ref, o_ref, gcol_ref, grow_ref, nw_ref, c0_ref, m0_ref,
                y_ref, c_out, m_out, c_s, m_s, *, chunk, dh):
    ci = pl.program_id(1)

    @pl.when(ci == 0)
    def _():
        c_s[...] = c0_ref[...]
        m_s[...] = m0_ref[...]

    row = lax.broadcasted_iota(jnp.int32, (chunk, chunk), 0)
    col = lax.broadcasted_iota(jnp.int32, (chunk, chunk), 1)
    causal = col <= row
    bcol = jnp.dot(causal.astype(F32), gcol_ref[...], precision=lax.Precision.HIGHEST,
                   preferred_element_type=F32)
    brow = jnp.dot(grow_ref[...], (row <= col).astype(F32), precision=lax.Precision.HIGHEST,
                   preferred_element_type=F32)
    eye = (lax.broadcasted_iota(jnp.int32, (dh, dh), 0) == lax.broadcasted_iota(jnp.int32, (dh, dh), 1)).astype(BF16)
    ones = jnp.ones((chunk, LANES), BF16)
    scale = dh ** -0.5
    for h in range(A_HEADS):
        sl = slice(h * dh, (h + 1) * dh)
        q = q_ref[:, sl]
        k = k_ref[:, sl] * scale
        v_aug = jnp.concatenate([v_ref[:, sl], ones], axis=1)
        b_c = bcol[:, A_HEADS + h:A_HEADS + h + 1]
        li_r = grow_ref[h:h + 1, :]
        b_r = brow[A_HEADS + h:A_HEADS + h + 1, :]
        b_end = b_c[chunk - 1:chunk, :]
        m_prev = m_s[h][:, 0:1]
        c_t = c_s[h]

        logd = jnp.where(causal, b_c - b_r + li_r, NEG)
        log_prev = b_c + m_prev
        m_t = jnp.maximum(log_prev, jnp.max(logd, axis=1, keepdims=True))
        d = jnp.exp(logd - m_t)
        w_prev = jnp.exp(log_prev - m_t)
        s = lax.dot_general(q, k, _NT, preferred_element_type=F32) * d
        num = (w_prev * jnp.dot(q, c_t.astype(BF16), preferred_element_type=F32)
               + jnp.dot(s.astype(BF16), v_aug, preferred_element_type=F32))
        den = num[:, dh:dh + 1]
        hh = num[:, :dh] / jnp.maximum(jnp.abs(den), jnp.exp(-m_t))
        ha = _sigmoid(o_ref[:, sl].astype(F32)) * hh
        y_ref[:, sl] = (_rms_scale(ha) * nw_ref[:, sl]).astype(y_ref.dtype)

        m_new = jnp.maximum(b_end + m_prev, jnp.max(b_end - b_r + li_r, axis=1, keepdims=True))
        w_r = jnp.exp(b_end - b_r + li_r - m_new)
        decay = jnp.exp(b_end + m_prev - m_new)
        kw_t = lax.dot_general(eye, k, _NT, preferred_element_type=F32) * w_r
        c_s[h] = decay * c_t + jnp.dot(kw_t.astype(BF16), v_aug, preferred_element_type=F32)
        m_s[h] = jnp.broadcast_to(m_new, (1, LANES))

    @pl.when(ci == pl.num_programs(1) - 1)
    def _():
        c_out[...] = c_s[...]
        m_out[...] = m_s[...]


def _mlstm(proj, gcol, grow, norm_w, state, *, nb, chunk, dh):
    n = gcol.shape[0]
    nc = n // nb // chunk
    width = A_HEADS * dh

    def tok(cb):
        return pl.BlockSpec((chunk, width), lambda b, c: (b * nc + c, cb))

    def whole(a):
        return pl.BlockSpec(a.shape, lambda b, c: (0,) * a.ndim)

    def per_seq(a):
        return pl.BlockSpec((None,) + a.shape, lambda b, c: (b,) + (0,) * a.ndim)

    nw = norm_w.reshape(1, width)
    in_specs = [tok(0), tok(1), tok(2), tok(3),
                pl.BlockSpec((chunk, LANES), lambda b, c: (b * nc + c, 0)),
                pl.BlockSpec((GATE_ROWS, chunk), lambda b, c: (0, b * nc + c)),
                whole(nw)] + [whole(a) for a in state]
    out_shape = [jax.ShapeDtypeStruct((n, width), BF16)] + [
        jax.ShapeDtypeStruct((nb,) + a.shape, F32) for a in state]
    out_specs = [pl.BlockSpec((chunk, width), lambda b, c: (b * nc + c, 0))] + [per_seq(a) for a in state]
    return pl.pallas_call(
        functools.partial(_mlstm_body, chunk=chunk, dh=dh),
        grid=(nb, nc), in_specs=in_specs, out_specs=out_specs, out_shape=out_shape,
        scratch_shapes=[pltpu.VMEM(a.shape, F32) for a in state],
        compiler_params=_cparams("parallel", "arbitrary"), name="mlstm",
    )(proj, proj, proj, proj, gcol, grow, nw, *state)


def _rglru_body(xb_ref, gate_ref, cw_ref, cb_ref, wri_ref, bri_ref, lam_ref, h0_ref, tail0_ref,
                y_ref, h_out, tail_out, h_s, tail_s, *, tt, bd):
    @pl.when(pl.program_id(1) == 0)
    def _():
        h_s[...] = h0_ref[...]
        tail_s[...] = tail0_ref[...]

    row = lax.broadcasted_iota(jnp.int32, (tt, bd), 0)
    row8 = row[:SUBLANES]
    for n in range(B_BLOCKS):
        sl = slice(n * bd, (n + 1) * bd)
        x = xb_ref[:, sl].astype(F32)
        prev_tail = tail_s[:, sl]
        xc = x * cw_ref[B_CONV - 1:B_CONV, sl] + cb_ref[:, sl]
        for s in range(1, B_CONV):
            xs = pltpu.roll(x, s, 0)
            head = jnp.where(row8 < s, pltpu.roll(prev_tail, s, 0), xs[:SUBLANES])
            xs = jnp.concatenate([head, xs[SUBLANES:]], axis=0)
            xc = xc + xs * cw_ref[B_CONV - 1 - s:B_CONV - s, sl]
        tail_s[:, sl] = x[tt - SUBLANES:, :]

        ri = jnp.dot(xc.astype(BF16), wri_ref[n], preferred_element_type=F32) + bri_ref[n]
        r = _sigmoid(ri[:, :bd])
        i = _sigmoid(ri[:, bd:])
        log_a = (-LRU_C * _softplus(-lam_ref[:, sl])) * r
        a = jnp.exp(log_a)
        u = jnp.sqrt(1.0 - a * a) * (i * xc)
        sh = 1
        while sh < tt:
            keep = row >= sh
            u = jnp.where(keep, a * pltpu.roll(u, sh, 0) + u, u)
            a = jnp.where(keep, a * pltpu.roll(a, sh, 0), a)
            sh *= 2
        hh = a * h_s[:, sl] + u
        h_s[:, sl] = hh[tt - 1:tt, :]
        y_ref[:, sl] = (hh * jax.nn.gelu(gate_ref[:, sl].astype(F32))).astype(y_ref.dtype)

    @pl.when(pl.program_id(1) == pl.num_programs(1) - 1)
    def _():
        h_out[...] = h_s[...]
        tail_out[...] = tail_s[...]


def _rglru(proj, xb_block, gate_block, conv_w, conv_b, w_ri, b_ri, lam, state, *, nb, tt):
    n = proj.shape[0]
    nt = n // nb // tt
    c = conv_w.shape[1]
    bd = c // B_BLOCKS
    h0, tail0 = state

    def whole(a):
        return pl.BlockSpec(a.shape, lambda b, t: (0,) * a.ndim)

    def per_seq(a):
        return pl.BlockSpec((None,) + a.shape, lambda b, t: (b,) + (0,) * a.ndim)

    consts = [conv_w, conv_b.reshape(1, c), w_ri, b_ri, lam.reshape(1, c), h0, tail0]
    in_specs = [pl.BlockSpec((tt, c), lambda b, t: (b * nt + t, xb_block)),
                pl.BlockSpec((tt, c), lambda b, t: (b * nt + t, gate_block))] + [whole(a) for a in consts]
    out_shape = [jax.ShapeDtypeStruct((n, c), BF16)] + [jax.ShapeDtypeStruct((nb,) + a.shape, F32) for a in state]
    out_specs = [pl.BlockSpec((tt, c), lambda b, t: (b * nt + t, 0))] + [per_seq(a) for a in state]
    return pl.pallas_call(
        functools.partial(_rglru_body, tt=tt, bd=bd),
        grid=(nb, nt), in_specs=in_specs, out_specs=out_specs, out_shape=out_shape,
        scratch_shapes=[pltpu.VMEM(a.shape, F32) for a in state],
        compiler_params=_cparams("parallel", "arbitrary"), name="rglru",
    )(proj, proj, *consts)


def _mix_mlp_body(*refs, n_y, final_norm, tf):
    it = iter(refs)
    x_ref = next(it)
    y_refs = [next(it) for _ in range(n_y)]
    wo_ref, g_ref, w1_ref, w2_ref = next(it), next(it), next(it), next(it)
    gf_ref = next(it) if final_norm else None
    o_ref, hn_s, acc_s = next(it), next(it), next(it)

    if n_y == 1:
        y = y_refs[0][...]
    else:
        ycat_s = next(it)
        off = 0
        for y_ref in y_refs:
            ycat_s[:, off:off + y_ref.shape[1]] = y_ref[...]
            off += y_ref.shape[1]
        y = ycat_s[...]
    acc_s[...] = x_ref[...] + jnp.dot(y, wo_ref[...], preferred_element_type=F32)
    hn_s[...] = (_rms_scale(acc_s[...]) * g_ref[...]).astype(BF16)
    for c in range(w1_ref.shape[1] // tf):
        a = jnp.maximum(jnp.dot(hn_s[...], w1_ref[:, c * tf:(c + 1) * tf], preferred_element_type=F32), 0.0)
        acc_s[...] += jnp.dot((a * a).astype(BF16), w2_ref[c * tf:(c + 1) * tf, :], preferred_element_type=F32)
    out = acc_s[...]
    if final_norm:
        out = _rms_scale(out) * gf_ref[...]
    o_ref[...] = out


def _mix_mlp(x, ys, wo, g, w1, w2, g_final=None, *, tm, tf):
    n, d = x.shape

    def rows(width):
        return pl.BlockSpec((tm, width), lambda i: (i, 0))

    in_specs = [rows(d)] + [rows(y.shape[1]) for y in ys]
    in_specs += [_resident(wo.shape), _resident((1, d)), _resident(w1.shape), _resident(w2.shape)]
    args = [x, *ys, wo, g.reshape(1, d), w1, w2]
    if g_final is not None:
        in_specs.append(_resident((1, d)))
        args.append(g_final.reshape(1, d))
    scratch = [pltpu.VMEM((tm, d), BF16), pltpu.VMEM((tm, d), F32)]
    if len(ys) > 1:
        scratch.append(pltpu.VMEM((tm, wo.shape[0]), BF16))
    return pl.pallas_call(
        functools.partial(_mix_mlp_body, n_y=len(ys), final_norm=g_final is not None, tf=tf),
        grid=(n // tm,), in_specs=in_specs, out_specs=rows(d),
        out_shape=jax.ShapeDtypeStruct((n, d), F32), scratch_shapes=scratch,
        compiler_params=_cparams("parallel"), name="mix_mlp")(*args)


def _diff_attn_body(q_ref, k_ref, v_ref, km_ref, vm_ref, lv_ref, sub_ref, o_ref, s_scr, sm_scr, mb_scr, lp_scr,
                    acc_scr, *, tq, dqk, hp, lambda_init):
    qi = pl.program_id(2)
    dv = 2 * dqk
    n_fold = tq // LANES
    lane_q = lax.broadcasted_iota(jnp.int32, (1, dv), 1)
    sel1 = (lane_q < dqk).astype(BF16)
    sel2 = (lane_q >= dqk).astype(BF16)
    meta_lane = lax.broadcasted_iota(jnp.int32, (2 * tq, LANES), 1) < N_META

    def head(ref, hh, rows=slice(None)):
        return ref[rows, hh * dv:(hh + 1) * dv]

    qs = []
    for hh in range(hp):
        q = head(q_ref, hh)
        qs.append(jnp.concatenate([q * sel1, q * sel2], axis=0))

    def scores(hh, k):
        return lax.dot_general(qs[hh], k, _NT, preferred_element_type=F32)

    def lane_fold(x, op):
        r = x[:, :LANES]
        for c in range(1, n_fold):
            r = op(r, x[:, c * LANES:(c + 1) * LANES])
        return r

    def key_block(j):
        return pl.ds(pl.multiple_of(j * tq, tq), tq)

    for hh in range(hp):
        s_meta = jnp.where(meta_lane, scores(hh, km_ref[hh]), NEG)
        sm_scr[hh] = s_meta
        mb_scr[hh] = s_meta

    def pass1(j, carry):
        for hh in range(hp):
            s = scores(hh, head(k_ref, hh, key_block(j)))
            s_scr[hh, j] = s
            mb_scr[hh] = jnp.maximum(mb_scr[hh], lane_fold(s, jnp.maximum))
        return carry

    lax.fori_loop(0, qi, pass1, 0)
    row = lax.broadcasted_iota(jnp.int32, (2 * tq, tq), 0)
    col = lax.broadcasted_iota(jnp.int32, (2 * tq, tq), 1)
    visible = col <= jnp.where(row >= tq, row - tq, row)
    for hh in range(hp):
        s_diag = jnp.where(visible, scores(hh, head(k_ref, hh, key_block(qi))), NEG)
        s_scr[hh, qi] = s_diag
        m_lanes = jnp.maximum(mb_scr[hh], lane_fold(s_diag, jnp.maximum))
        mb = jnp.broadcast_to(jnp.max(m_lanes, axis=1, keepdims=True), m_lanes.shape)
        mb_scr[hh] = mb
        p_meta = jnp.exp2(sm_scr[hh] - mb)
        lp_scr[hh] = p_meta
        acc_scr[hh] = jnp.dot(p_meta.astype(BF16), vm_ref[hh], preferred_element_type=F32)

    def pass2(j, carry):
        for hh in range(hp):
            mb = mb_scr[hh]
            p = jnp.concatenate(
                [jnp.exp2(s_scr[hh, j, :, c * LANES:(c + 1) * LANES] - mb) for c in range(n_fold)], axis=1)
            lp_scr[hh] += lane_fold(p, jnp.add)
            acc_scr[hh] += jnp.dot(p.astype(BF16), head(v_ref, hh, key_block(j)), preferred_element_type=F32)
        return carry

    lax.fori_loop(0, qi + 1, pass2, 0)
    lv = lv_ref[...]
    lam = (jnp.exp(jnp.sum(lv[0:1] * lv[1:2], axis=1, keepdims=True))
           - jnp.exp(jnp.sum(lv[2:3] * lv[3:4], axis=1, keepdims=True)) + lambda_init)
    for hh in range(hp):
        o = acc_scr[hh] / jnp.sum(lp_scr[hh], axis=1, keepdims=True)
        od = o[:tq] - lam * o[tq:]
        od = _rms_scale(od) * sub_ref[...] * (1.0 - lambda_init)
        o_ref[:, hh * dv:(hh + 1) * dv] = od.astype(o_ref.dtype)


def _diff_attn(qkv, k_meta, v_meta, lam_vecs, subln, *, nb, seq, heads, tq, hp, lambda_init):
    n = qkv.shape[0]
    dv = subln.shape[0]
    nq = seq // tq
    hg = heads // hp
    w = hp * dv
    body = functools.partial(_diff_attn_body, tq=tq, dqk=dv // 2, hp=hp, lambda_init=lambda_init)
    stat = pltpu.VMEM((hp, 2 * tq, LANES), F32)
    return pl.pallas_call(
        body, grid=(nb, hg, nq),
        in_specs=[pl.BlockSpec((tq, w), lambda b, h, i: (b * nq + i, h)),
                  pl.BlockSpec((seq, w), lambda b, h, i: (b, hg + h)),
                  pl.BlockSpec((seq, w), lambda b, h, i: (b, 2 * hg + h)),
                  pl.BlockSpec((hp, LANES, dv), lambda b, h, i: (h, 0, 0)),
                  pl.BlockSpec((hp, LANES, dv), lambda b, h, i: (h, 0, 0)),
                  pl.BlockSpec(lam_vecs.shape, lambda b, h, i: (0, 0)),
                  pl.BlockSpec((1, dv), lambda b, h, i: (0, 0))],
        out_specs=pl.BlockSpec((tq, w), lambda b, h, i: (b * nq + i, h)),
        out_shape=jax.ShapeDtypeStruct((n, heads * dv), BF16),
        scratch_shapes=[pltpu.VMEM((hp, nq, 2 * tq, tq), F32), stat, stat, stat, stat],
        compiler_params=_cparams("parallel", "parallel", "arbitrary"), name="diff_attn",
    )(qkv, qkv, qkv, k_meta, v_meta, lam_vecs, subln.reshape(1, dv))


def _rope_tables(positions, dqk, rot_dim):
    half = rot_dim // 2
    inv_freq = jnp.power(jnp.float32(ROPE_THETA), -jnp.arange(0, rot_dim, 2, dtype=F32) / rot_dim)
    ang = positions.astype(F32)[:, None] * inv_freq[None, :]
    c = jnp.arange(LANES) % dqk
    idx = c % half
    cos = jnp.where(c < rot_dim, jnp.cos(ang)[:, idx], 1.0)
    sin = jnp.sin(ang)[:, idx]
    sa = jnp.where(c < half, -sin, 0.0)
    sb = jnp.where((c >= half) & (c < rot_dim), sin, 0.0)
    qs = dqk ** -0.5 * math.log2(math.e)
    return jnp.stack([cos * qs, cos]), jnp.stack([sa * qs, sa]), jnp.stack([sb * qs, sb])


def kernel(x, meta_tokens, norm_mix, norm_mlp, norm_final, ab_w_in, ab_if_bias, mlstm_norm, lru_conv_w,
           lru_conv_b, lru_w_r, lru_b_r, lru_w_i, lru_b_i, lru_lambda, ab_w_out, c_w_in, c_lambda,
           c_subln, c_w_out, mlp_w1, mlp_w2):
    bsz, seq, d = x.shape
    n = bsz * seq
    assert norm_mix.shape[0] == 2 and meta_tokens.shape[0] == N_META
    a_width = mlstm_norm.shape[1]
    dh = a_width // A_HEADS
    b_width = lru_conv_w.shape[2]
    bd = b_width // B_BLOCKS
    assert a_width == b_width == d
    c_heads = d // LANES
    dv = c_subln.shape[1]
    dqk = dv // 2
    rot_dim = dqk // 4

    tm_proj = _pick_tile(n, 512)
    chunk = _pick_tile(seq, 256)
    tt = _pick_tile(seq, 256)
    tm_mlp = _pick_tile(n, 512)
    tf = 1024
    tq = _pick_tile(seq, 512)

    xf = x.reshape(n, d)
    meta = meta_tokens.astype(x.dtype)

    w_in = ab_w_in[0]
    n_qkvo = 4 * a_width
    w_main = jnp.concatenate([w_in[:, :n_qkvo], w_in[:, n_qkvo + 2 * A_HEADS:]], axis=1).astype(BF16)
    w_g = w_in[:, n_qkvo:n_qkvo + 2 * A_HEADS]
    wg_col = jnp.pad(w_g, ((0, 0), (0, LANES - 2 * A_HEADS))).astype(BF16)
    wg_row = jnp.pad(w_g.T, ((0, GATE_ROWS - 2 * A_HEADS), (0, 0))).astype(BF16)
    ifb = ab_if_bias[0].astype(F32)
    gates = (wg_col, wg_row, jnp.pad(ifb, (0, LANES - 2 * A_HEADS)).reshape(1, LANES),
             jnp.pad(ifb, (0, GATE_ROWS - 2 * A_HEADS)).reshape(GATE_ROWS, 1))
    w_ri = jnp.concatenate([lru_w_r[0], lru_w_i[0]], axis=-1).astype(BF16)
    b_ri = jnp.concatenate([lru_b_r[0].reshape(B_BLOCKS, 1, bd), lru_b_i[0].reshape(B_BLOCKS, 1, bd)],
                           axis=-1).astype(F32)
    wo_ab = ab_w_out[0].astype(BF16)
    w1 = [mlp_w1[l].astype(BF16) for l in range(2)]
    w2 = [mlp_w2[l].astype(BF16) for l in range(2)]

    def layer0(rows, nb, t_rows, tm, chunk_, tt_, state_a, state_b):
        proj, gcol, grow = _norm_proj(rows, norm_mix[0], w_main, tm=tm, tn=512, gates=gates)
        y_a, *fin_a = _mlstm(proj, gcol, grow, mlstm_norm[0], state_a, nb=nb, chunk=chunk_, dh=dh)
        y_b, *fin_b = _rglru(proj, 4, 5, lru_conv_w[0], lru_conv_b[0], w_ri, b_ri, lru_lambda[0], state_b,
                             nb=nb, tt=tt_)
        h = _mix_mlp(rows, [y_a, y_b], wo_ab, norm_mlp[0], w1[0], w2[0], tm=tm, tf=tf)
        return h, fin_a, fin_b

    zero_a = (jnp.zeros((A_HEADS, dh, dh + LANES), F32), jnp.zeros((A_HEADS, 1, LANES), F32))
    zero_b = (jnp.zeros((1, b_width), F32), jnp.zeros((SUBLANES, b_width), F32))
    h_meta, fin_a, fin_b = layer0(meta, 1, N_META, N_META, N_META, N_META, zero_a, zero_b)
    h_real, _, _ = layer0(xf, bsz, seq, tm_proj, chunk, tt, tuple(a[0] for a in fin_a),
                          tuple(a[0] for a in fin_b))

    lambda_init = 0.8 - 0.6 * math.exp(-0.3 * 1)
    w_qkv = c_w_in[0].astype(BF16)
    pos = jnp.arange(N_META + seq, dtype=jnp.int32)
    rope_meta = _rope_tables(pos[:N_META], dqk, rot_dim)
    rope_real = _rope_tables(pos[N_META:], dqk, rot_dim)
    qkv_meta = _norm_proj(h_meta, norm_mix[1], w_qkv, tm=N_META, tn=512, rope=rope_meta, rope_cols=d)
    def meta_block(a):
        a = a.reshape(N_META, c_heads, dv).transpose(1, 0, 2)
        return jnp.pad(a, ((0, 0), (0, LANES - N_META), (0, 0)))

    k_meta = meta_block(qkv_meta[:, d:2 * d])
    v_meta = meta_block(qkv_meta[:, 2 * d:])
    qkv = _norm_proj(h_real, norm_mix[1], w_qkv, tm=_pick_tile(seq, 512), tn=512, rope=rope_real, rope_cols=d)
    o = _diff_attn(qkv, k_meta, v_meta, c_lambda[0].astype(F32), c_subln[0].astype(F32), nb=bsz, seq=seq,
                   heads=c_heads, tq=tq, hp=2, lambda_init=lambda_init)
    out = _mix_mlp(h_real, [o], c_w_out[0].astype(BF16), norm_mlp[1], w1[1], w2[1], norm_final,
                   tm=tm_mlp, tf=tf)
    return out.reshape(bsz, seq, d)
```

```python
import functools
import math

import jax
import jax.numpy as jnp
from jax import lax
from jax.experimental import pallas as pl
from jax.experimental.pallas import tpu as pltpu

F32 = jnp.float32
BF16 = jnp.bfloat16

N_META = 16
EPS = 1e-6
A_HEADS = 4
B_BLOCKS = 8
B_CONV = 4
LRU_C = 8.0
ROPE_THETA = 500000.0

LANES = 128
SUBLANES = 8
GATE_ROWS = 16
NEG = -1e30
VMEM_LIMIT_BYTES = 56 * 1024 * 1024

_NT = (((1,), (1,)), ((), ()))
_TN = (((0,), (0,)), ((), ()))


def _cparams(*sem):
    return pltpu.CompilerParams(dimension_semantics=sem, vmem_limit_bytes=VMEM_LIMIT_BYTES)


def _sigmoid(x):
    return 0.5 * jnp.tanh(0.5 * x) + 0.5


def _sqrt_nonneg(x):
    return jnp.where(x > 0.0, x * lax.rsqrt(x), 0.0)


def _log_sigmoid(x):
    return jnp.minimum(x, 0.0) - jnp.log1p(jnp.exp(-jnp.abs(x)))


def _softplus(x):
    return jnp.maximum(x, 0.0) + jnp.log1p(jnp.exp(-jnp.abs(x)))


def _rms_scale(x):
    return x * lax.rsqrt(jnp.mean(x * x, axis=-1, keepdims=True) + EPS)


def _pick_tile(n, pref):
    t = min(n, pref)
    while n % t:
        t //= 2
    return t


def _resident(shape):
    return pl.BlockSpec(shape, lambda *_: (0,) * len(shape), pipeline_mode=pl.Buffered(1))


def _norm_proj_body(*refs, has_rope, has_gates, tn, rope_cols):
    it = iter(refs)
    x_ref, g_ref, w_ref = next(it), next(it), next(it)
    if has_rope:
        cos_ref, sa_ref, sb_ref = next(it), next(it), next(it)
    if has_gates:
        wgc_ref, wgr_ref, bc_ref, br_ref = next(it), next(it), next(it), next(it)
    o_ref = next(it)
    if has_gates:
        gcol_ref, grow_ref = next(it), next(it)
    xn_ref = next(it)

    xn = (_rms_scale(x_ref[...]) * g_ref[...]).astype(BF16)
    xn_ref[...] = xn
    if has_gates:
        gc = jnp.dot(xn, wgc_ref[...], preferred_element_type=F32) + bc_ref[...]
        lane = lax.broadcasted_iota(jnp.int32, gc.shape, 1)
        gcol_ref[...] = jnp.where(lane < A_HEADS, gc, _log_sigmoid(gc))
        gr = lax.dot_general(wgr_ref[...], xn, _NT, preferred_element_type=F32) + br_ref[...]
        sub = lax.broadcasted_iota(jnp.int32, gr.shape, 0)
        grow_ref[...] = jnp.where(sub < A_HEADS, gr, _log_sigmoid(gr))

    for c in range(o_ref.shape[1] // tn):
        acc = jnp.dot(xn_ref[...], w_ref[:, c * tn:(c + 1) * tn], preferred_element_type=F32)
        sec = (c * tn) // rope_cols if has_rope else None
        if has_rope and sec < cos_ref.shape[0]:
            cos, sa, sb = cos_ref[sec], sa_ref[sec], sb_ref[sec]
            for cc in range(tn // LANES):
                y = acc[:, cc * LANES:(cc + 1) * LANES]
                y = y * cos + pltpu.roll(y, LANES - 8, 1) * sa + pltpu.roll(y, 8, 1) * sb
                o_ref[:, c * tn + cc * LANES:c * tn + (cc + 1) * LANES] = y.astype(o_ref.dtype)
        else:
            o_ref[:, c * tn:(c + 1) * tn] = acc.astype(o_ref.dtype)


def _norm_proj(x, g, w, *, tm, tn, rope=None, rope_cols=None, gates=None):
    n, d = x.shape
    nout = w.shape[1]
    period = rope[0].shape[1] if rope is not None else n
    p_tiles = period // tm
    grid = (p_tiles, n // period)

    def rows(width):
        return pl.BlockSpec((tm, width), lambda p, r: (r * p_tiles + p, 0))

    in_specs = [rows(d), _resident((1, d)), _resident(w.shape)]
    args = [x, g.reshape(1, d), w]
    if rope is not None:
        spec = pl.BlockSpec((rope[0].shape[0], tm, LANES), lambda p, r: (0, p, 0))
        in_specs += [spec, spec, spec]
        args += list(rope)
    out_shape = [jax.ShapeDtypeStruct((n, nout), BF16)]
    out_specs = [rows(nout)]
    if gates is not None:
        in_specs += [_resident(a.shape) for a in gates]
        args += list(gates)
        out_shape += [jax.ShapeDtypeStruct((n, LANES), F32), jax.ShapeDtypeStruct((GATE_ROWS, n), F32)]
        out_specs += [rows(LANES), pl.BlockSpec((GATE_ROWS, tm), lambda p, r: (0, r * p_tiles + p))]
    body = functools.partial(_norm_proj_body, has_rope=rope is not None, has_gates=gates is not None, tn=tn,
                             rope_cols=rope_cols)
    outs = pl.pallas_call(
        body, grid=grid, in_specs=in_specs, out_specs=out_specs, out_shape=out_shape,
        scratch_shapes=[pltpu.VMEM((tm, d), BF16)],
        compiler_params=_cparams("parallel", "parallel"), name="norm_proj")(*args)
    return outs if gates is not None else outs[0]


def _mlstm_body(q_ref, k_ref, v_ref, o_ref, gcol_ref, grow_ref, nw_ref, c0_ref, m0_ref,
                y_ref, c_out, m_out, c_s, m_s, *, chunk, dh):
    ci = pl.program_id(1)

    @pl.when(ci == 0)
    def _():
        c_s[...] = c0_ref[...]
        m_s[...] = m0_ref[...]

    row = lax.broadcasted_iota(jnp.int32, (chunk, chunk), 0)
    col = lax.broadcasted_iota(jnp.int32, (chunk, chunk), 1)
    causal = col <= row
    bcol = jnp.dot(causal.astype(F32), gcol_ref[...], precision=lax.Precision.HIGHEST,
                   preferred_element_type=F32)
    brow = jnp.dot(grow_ref[...], (row <= col).astype(F32), precision=lax.Precision.HIGHEST,
                   preferred_element_type=F32)
    eye = (lax.broadcasted_iota(jnp.int32, (dh, dh), 0) == lax.broadcasted_iota(jnp.int32, (dh, dh), 1)).astype(BF16)
    ones = jnp.ones((chunk, LANES), BF16)
    scale = dh ** -0.5
    for h in range(A_HEADS):
        sl = slice(h * dh, (h + 1) * dh)
        q = q_ref[:, sl]
        k = k_ref[:, sl] * scale
        v_aug = jnp.concatenate([v_ref[:, sl], ones], axis=1)
        b_c = bcol[:, A_HEADS + h:A_HEADS + h + 1]
        li_r = grow_ref[h:h + 1, :]
        b_r = brow[A_HEADS + h:A_HEADS + h + 1, :]
        b_end = b_c[chunk - 1:chunk, :]
        m_prev = m_s[h][:, 0:1]
        c_t = c_s[h]

        logd = jnp.where(causal, b_c - b_r + li_r, NEG)
        log_prev = b_c + m_prev
        m_t = jnp.maximum(log_prev, jnp.max(logd, axis=1, keepdims=True))
        d = jnp.exp(logd - m_t)
        w_prev = jnp.exp(log_prev - m_t)
        s = lax.dot_general(q, k, _NT, preferred_element_type=F32) * d
        num = (w_prev * jnp.dot(q, c_t.astype(BF16), preferred_element_type=F32)
               + jnp.dot(s.astype(BF16), v_aug, preferred_element_type=F32))
        den = num[:, dh:dh + 1]
        hh = num[:, :dh] / jnp.maximum(jnp.abs(den), jnp.exp(-m_t))
        ha = _sigmoid(o_ref[:, sl].astype(F32)) * hh
        y_ref[:, sl] = (_rms_scale(ha) * nw_ref[:, sl]).astype(y_ref.dtype)

        m_new = jnp.maximum(b_end + m_prev, jnp.max(b_end - b_r + li_r, axis=1, keepdims=True))
        w_r = jnp.exp(b_end - b_r + li_r - m_new)
        decay = jnp.exp(b_end + m_prev - m_new)
        kw_t = lax.dot_general(eye, k, _NT, preferred_element_type=F32) * w_r
        c_s[h] = decay * c_t + jnp.dot(kw_t.astype(BF16), v_aug, preferred_element_type=F32)
        m_s[h] = jnp.broadcast_to(m_new, (1, LANES))

    @pl.when(ci == pl.num_programs(1) - 1)
    def _():
        c_out[...] = c_s[...]
        m_out[...] = m_s[...]


def _mlstm(proj, gcol, grow, norm_w, state, *, nb, chunk, dh):
    n = gcol.shape[0]
    nc = n // nb // chunk
    width = A_HEADS * dh

    def tok(cb):
        return pl.BlockSpec((chunk, width), lambda b, c: (b * nc + c, cb))

    def whole(a):
        return pl.BlockSpec(a.shape, lambda b, c: (0,) * a.ndim)

    def per_seq(a):
        return pl.BlockSpec((None,) + a.shape, lambda b, c: (b,) + (0,) * a.ndim)

    nw = norm_w.reshape(1, width)
    in_specs = [tok(0), tok(1), tok(2), tok(3),
                pl.BlockSpec((chunk, LANES), lambda b, c: (b * nc + c, 0)),
                pl.BlockSpec((GATE_ROWS, chunk), lambda b, c: (0, b * nc + c)),
                whole(nw)] + [whole(a) for a in state]
    out_shape = [jax.ShapeDtypeStruct((n, width), BF16)] + [
        jax.ShapeDtypeStruct((nb,) + a.shape, F32) for a in state]
    out_specs = [pl.BlockSpec((chunk, width), lambda b, c: (b * nc + c, 0))] + [per_seq(a) for a in state]
    return pl.pallas_call(
        functools.partial(_mlstm_body, chunk=chunk, dh=dh),
        grid=(nb, nc), in_specs=in_specs, out_specs=out_specs, out_shape=out_shape,
        scratch_shapes=[pltpu.VMEM(a.shape, F32) for a in state],
        compiler_params=_cparams("parallel", "arbitrary"), name="mlstm",
    )(proj, proj, proj, proj, gcol, grow, nw, *state)


def _rglru_body(xb_ref, gate_ref, cw_ref, cb_ref, wri_ref, bri_ref, lam_ref, h0_ref, tail0_ref,
                y_ref, h_out, tail_out, h_s, xpad_s, *, tt, bd):
    @pl.when(pl.program_id(1) == 0)
    def _():
        h_s[...] = h0_ref[...]
        xpad_s[:SUBLANES, :] = tail0_ref[...]

    xpad_s[SUBLANES:, :] = xb_ref[...].astype(F32)
    row = lax.broadcasted_iota(jnp.int32, (tt, bd), 0)
    for n in range(B_BLOCKS):
        sl = slice(n * bd, (n + 1) * bd)
        xc = cb_ref[:, sl]
        for s in range(B_CONV):
            xc = xc + xpad_s[SUBLANES - s:SUBLANES - s + tt, sl] * cw_ref[B_CONV - 1 - s:B_CONV - s, sl]

        ri = jnp.dot(xc.astype(BF16), wri_ref[n], preferred_element_type=F32) + bri_ref[n]
        r = _sigmoid(ri[:, :bd])
        i = _sigmoid(ri[:, bd:])
        log_a = (-LRU_C * _softplus(-lam_ref[:, sl])) * r
        a = jnp.exp(log_a)
        u = _sqrt_nonneg(1.0 - a * a) * (i * xc)
        sh = 1
        while sh < tt:
            if sh < SUBLANES:
                keep = row >= sh
                u = jnp.where(keep, a * pltpu.roll(u, sh, 0) + u, u)
                a = jnp.where(keep, a * pltpu.roll(a, sh, 0), a)
            else:
                u = jnp.concatenate([u[:sh], a[sh:] * u[:tt - sh] + u[sh:]], axis=0)
                a = jnp.concatenate([a[:sh], a[sh:] * a[:tt - sh]], axis=0)
            sh *= 2
        hh = a * h_s[:, sl] + u
        h_s[:, sl] = hh[tt - 1:tt, :]
        y_ref[:, sl] = (hh * jax.nn.gelu(gate_ref[:, sl].astype(F32))).astype(y_ref.dtype)

    last8 = xpad_s[tt:, :]
    xpad_s[:SUBLANES, :] = last8

    @pl.when(pl.program_id(1) == pl.num_programs(1) - 1)
    def _():
        h_out[...] = h_s[...]
        tail_out[...] = last8


def _rglru(proj, xb_block, gate_block, conv_w, conv_b, w_ri, b_ri, lam, state, *, nb, tt):
    n = proj.shape[0]
    nt = n // nb // tt
    c = conv_w.shape[1]
    bd = c // B_BLOCKS
    h0, tail0 = state

    def whole(a):
        return pl.BlockSpec(a.shape, lambda b, t: (0,) * a.ndim)

    def per_seq(a):
        return pl.BlockSpec((None,) + a.shape, lambda b, t: (b,) + (0,) * a.ndim)

    consts = [conv_w, conv_b.reshape(1, c), w_ri, b_ri, lam.reshape(1, c), h0, tail0]
    in_specs = [pl.BlockSpec((tt, c), lambda b, t: (b * nt + t, xb_block)),
                pl.BlockSpec((tt, c), lambda b, t: (b * nt + t, gate_block))] + [whole(a) for a in consts]
    out_shape = [jax.ShapeDtypeStruct((n, c), BF16)] + [jax.ShapeDtypeStruct((nb,) + a.shape, F32) for a in state]
    out_specs = [pl.BlockSpec((tt, c), lambda b, t: (b * nt + t, 0))] + [per_seq(a) for a in state]
    return pl.pallas_call(
        functools.partial(_rglru_body, tt=tt, bd=bd),
        grid=(nb, nt), in_specs=in_specs, out_specs=out_specs, out_shape=out_shape,
        scratch_shapes=[pltpu.VMEM(h0.shape, F32), pltpu.VMEM((SUBLANES + tt, c), F32)],
        compiler_params=_cparams("parallel", "arbitrary"), name="rglru",
    )(proj, proj, *consts)


def _mix_mlp_body(*refs, n_y, final_norm, tf):
    it = iter(refs)
    x_ref = next(it)
    y_refs = [next(it) for _ in range(n_y)]
    wo_ref, g_ref, w1_ref, w2_ref = next(it), next(it), next(it), next(it)
    gf_ref = next(it) if final_norm else None
    o_ref, hn_s, acc_s = next(it), next(it), next(it)

    if n_y == 1:
        y = y_refs[0][...]
    else:
        ycat_s = next(it)
        off = 0
        for y_ref in y_refs:
            ycat_s[:, off:off + y_ref.shape[1]] = y_ref[...]
            off += y_ref.shape[1]
        y = ycat_s[...]
    acc_s[...] = x_ref[...] + jnp.dot(y, wo_ref[...], preferred_element_type=F32)
    hn_s[...] = (_rms_scale(acc_s[...]) * g_ref[...]).astype(BF16)
    for c in range(w1_ref.shape[1] // tf):
        a = jnp.maximum(jnp.dot(hn_s[...], w1_ref[:, c * tf:(c + 1) * tf], preferred_element_type=F32), 0.0)
        acc_s[...] += jnp.dot((a * a).astype(BF16), w2_ref[c * tf:(c + 1) * tf, :], preferred_element_type=F32)
    out = acc_s[...]
    if final_norm:
        out = _rms_scale(out) * gf_ref[...]
    o_ref[...] = out


def _mix_mlp(x, ys, wo, g, w1, w2, g_final=None, *, tm, tf):
    n, d = x.shape

    def rows(width):
        return pl.BlockSpec((tm, width), lambda i: (i, 0))

    in_specs = [rows(d)] + [rows(y.shape[1]) for y in ys]
    in_specs += [_resident(wo.shape), _resident((1, d)), _resident(w1.shape), _resident(w2.shape)]
    args = [x, *ys, wo, g.reshape(1, d), w1, w2]
    if g_final is not None:
        in_specs.append(_resident((1, d)))
        args.append(g_final.reshape(1, d))
    scratch = [pltpu.VMEM((tm, d), BF16), pltpu.VMEM((tm, d), F32)]
    if len(ys) > 1:
        scratch.append(pltpu.VMEM((tm, wo.shape[0]), BF16))
    return pl.pallas_call(
        functools.partial(_mix_mlp_body, n_y=len(ys), final_norm=g_final is not None, tf=tf),
        grid=(n // tm,), in_specs=in_specs, out_specs=rows(d),
        out_shape=jax.ShapeDtypeStruct((n, d), F32), scratch_shapes=scratch,
        compiler_params=_cparams("parallel"), name="mix_mlp")(*args)


def _diff_attn_body(q_ref, k_ref, v_ref, km_ref, vm_ref, lv_ref, sub_ref, o_ref, s_scr, sm_scr, mb_scr, acc_scr,
                    *, tq, dqk, hp, lambda_init):
    qi = pl.program_id(2)
    dv = 2 * dqk
    n_fold = tq // LANES
    lane_q = lax.broadcasted_iota(jnp.int32, (1, dv), 1)
    sel1 = (lane_q < dqk).astype(BF16)
    sel2 = (lane_q >= dqk).astype(BF16)
    meta_lane = lax.broadcasted_iota(jnp.int32, (2 * tq, LANES), 1) < N_META

    def head(ref, hh, rows=slice(None)):
        return ref[rows, hh * dv:(hh + 1) * dv]

    qs = []
    for hh in range(hp):
        q = head(q_ref, hh)
        qs.append(jnp.concatenate([q * sel1, q * sel2], axis=0))

    def scores(hh, k):
        return lax.dot_general(qs[hh], k, _NT, preferred_element_type=F32)

    def lane_fold(x, op):
        r = x[:, :LANES]
        for c in range(1, x.shape[1] // LANES):
            r = op(r, x[:, c * LANES:(c + 1) * LANES])
        return r

    def key_block(j, nblk=1):
        return pl.ds(pl.multiple_of(j * tq, tq), nblk * tq)

    def with_ones(v):
        return jnp.concatenate([v, jnp.ones((v.shape[0], LANES), v.dtype)], axis=1)

    for hh in range(hp):
        s_meta = jnp.where(meta_lane, scores(hh, km_ref[hh]), NEG)
        sm_scr[hh] = s_meta
        mb_scr[hh] = s_meta

    def pass1(j, nblk):
        for hh in range(hp):
            s = scores(hh, head(k_ref, hh, key_block(j, nblk)))
            for b in range(nblk):
                s_scr[hh, j + b] = s[:, b * tq:(b + 1) * tq]
            mb_scr[hh] = jnp.maximum(mb_scr[hh], lane_fold(s, jnp.maximum))

    def in_pairs(n_blocks, fn):
        def pair(t, carry):
            fn(2 * t, 2)
            return carry

        lax.fori_loop(0, n_blocks // 2, pair, 0)

        @pl.when(n_blocks % 2 == 1)
        def _():
            fn(n_blocks - 1, 1)

    in_pairs(qi, pass1)
    row = lax.broadcasted_iota(jnp.int32, (2 * tq, tq), 0)
    col = lax.broadcasted_iota(jnp.int32, (2 * tq, tq), 1)
    visible = col <= jnp.where(row >= tq, row - tq, row)
    for hh in range(hp):
        s_diag = jnp.where(visible, scores(hh, head(k_ref, hh, key_block(qi))), NEG)
        s_scr[hh, qi] = s_diag
        m_lanes = jnp.maximum(mb_scr[hh], lane_fold(s_diag, jnp.maximum))
        mb = jnp.broadcast_to(jnp.max(m_lanes, axis=1, keepdims=True), m_lanes.shape)
        mb_scr[hh] = mb
        p_meta = jnp.exp2(sm_scr[hh] - mb)
        acc_scr[hh] = jnp.dot(p_meta.astype(BF16), with_ones(vm_ref[hh]), preferred_element_type=F32)

    def pass2(j, nblk):
        for hh in range(hp):
            mb = mb_scr[hh]
            p = jnp.concatenate(
                [jnp.exp2(s_scr[hh, j + b, :, c * LANES:(c + 1) * LANES] - mb)
                 for b in range(nblk) for c in range(n_fold)], axis=1)
            acc_scr[hh] += jnp.dot(p.astype(BF16), with_ones(head(v_ref, hh, key_block(j, nblk))),
                                   preferred_element_type=F32)

    in_pairs(qi + 1, pass2)
    lv = lv_ref[...]
    lam = (jnp.exp(jnp.sum(lv[0:1] * lv[1:2], axis=1, keepdims=True))
           - jnp.exp(jnp.sum(lv[2:3] * lv[3:4], axis=1, keepdims=True)) + lambda_init)
    for hh in range(hp):
        acc = acc_scr[hh]
        o = acc[:, :dv] / acc[:, dv:]
        od = o[:tq] - lam * o[tq:]
        od = _rms_scale(od) * sub_ref[...] * (1.0 - lambda_init)
        o_ref[:, hh * dv:(hh + 1) * dv] = od.astype(o_ref.dtype)


def _diff_attn(qkv, k_meta, v_meta, lam_vecs, subln, *, nb, seq, heads, tq, hp, lambda_init):
    n = qkv.shape[0]
    dv = subln.shape[0]
    nq = seq // tq
    hg = heads // hp
    w = hp * dv
    body = functools.partial(_diff_attn_body, tq=tq, dqk=dv // 2, hp=hp, lambda_init=lambda_init)
    stat = pltpu.VMEM((hp, 2 * tq, LANES), F32)
    acc = pltpu.VMEM((hp, 2 * tq, dv + LANES), F32)
    return pl.pallas_call(
        body, grid=(nb, hg, nq),
        in_specs=[pl.BlockSpec((tq, w), lambda b, h, i: (b * nq + i, h)),
                  pl.BlockSpec((seq, w), lambda b, h, i: (b, hg + h)),
                  pl.BlockSpec((seq, w), lambda b, h, i: (b, 2 * hg + h)),
                  pl.BlockSpec((hp, LANES, dv), lambda b, h, i: (h, 0, 0)),
                  pl.BlockSpec((hp, LANES, dv), lambda b, h, i: (h, 0, 0)),
                  pl.BlockSpec(lam_vecs.shape, lambda b, h, i: (0, 0)),
                  pl.BlockSpec((1, dv), lambda b, h, i: (0, 0))],
        out_specs=pl.BlockSpec((tq, w), lambda b, h, i: (b * nq + i, h)),
        out_shape=jax.ShapeDtypeStruct((n, heads * dv), BF16),
        scratch_shapes=[pltpu.VMEM((hp, nq, 2 * tq, tq), F32), stat, stat, acc],
        compiler_params=_cparams("parallel", "parallel", "arbitrary"), name="diff_attn",
    )(qkv, qkv, qkv, k_meta, v_meta, lam_vecs, subln.reshape(1, dv))


def _rope_tables(positions, dqk, rot_dim):
    half = rot_dim // 2
    inv_freq = jnp.power(jnp.float32(ROPE_THETA), -jnp.arange(0, rot_dim, 2, dtype=F32) / rot_dim)
    ang = positions.astype(F32)[:, None] * inv_freq[None, :]
    c = jnp.arange(LANES) % dqk
    idx = c % half
    cos = jnp.where(c < rot_dim, jnp.cos(ang)[:, idx], 1.0)
    sin = jnp.sin(ang)[:, idx]
    sa = jnp.where(c < half, -sin, 0.0)
    sb = jnp.where((c >= half) & (c < rot_dim), sin, 0.0)
    qs = dqk ** -0.5 * math.log2(math.e)
    return jnp.stack([cos * qs, cos]), jnp.stack([sa * qs, sa]), jnp.stack([sb * qs, sb])


def kernel(x, meta_tokens, norm_mix, norm_mlp, norm_final, ab_w_in, ab_if_bias, mlstm_norm, lru_conv_w,
           lru_conv_b, lru_w_r, lru_b_r, lru_w_i, lru_b_i, lru_lambda, ab_w_out, c_w_in, c_lambda,
           c_subln, c_w_out, mlp_w1, mlp_w2):
    bsz, seq, d = x.shape
    n = bsz * seq
    assert norm_mix.shape[0] == 2 and meta_tokens.shape[0] == N_META
    a_width = mlstm_norm.shape[1]
    dh = a_width // A_HEADS
    b_width = lru_conv_w.shape[2]
    bd = b_width // B_BLOCKS
    assert a_width == b_width == d
    c_heads = d // LANES
    dv = c_subln.shape[1]
    dqk = dv // 2
    rot_dim = dqk // 4

    tm_proj = _pick_tile(n, 512)
    chunk = _pick_tile(seq, 256)
    tt = _pick_tile(seq, 256)
    tm_mlp = _pick_tile(n, 512)
    tf = 1024
    tq = _pick_tile(seq, 512)

    xf = x.reshape(n, d)
    meta = meta_tokens.astype(x.dtype)

    w_in = ab_w_in[0]
    n_qkvo = 4 * a_width
    w_main = jnp.concatenate([w_in[:, :n_qkvo], w_in[:, n_qkvo + 2 * A_HEADS:]], axis=1).astype(BF16)
    w_g = w_in[:, n_qkvo:n_qkvo + 2 * A_HEADS]
    wg_col = jnp.pad(w_g, ((0, 0), (0, LANES - 2 * A_HEADS))).astype(BF16)
    wg_row = jnp.pad(w_g.T, ((0, GATE_ROWS - 2 * A_HEADS), (0, 0))).astype(BF16)
    ifb = ab_if_bias[0].astype(F32)
    gates = (wg_col, wg_row, jnp.pad(ifb, (0, LANES - 2 * A_HEADS)).reshape(1, LANES),
             jnp.pad(ifb, (0, GATE_ROWS - 2 * A_HEADS)).reshape(GATE_ROWS, 1))
    w_ri = jnp.concatenate([lru_w_r[0], lru_w_i[0]], axis=-1).astype(BF16)
    b_ri = jnp.concatenate([lru_b_r[0].reshape(B_BLOCKS, 1, bd), lru_b_i[0].reshape(B_BLOCKS, 1, bd)],
                           axis=-1).astype(F32)
    wo_ab = ab_w_out[0].astype(BF16)
    w1 = [mlp_w1[l].astype(BF16) for l in range(2)]
    w2 = [mlp_w2[l].astype(BF16) for l in range(2)]

    def layer0(rows, nb, t_rows, tm, chunk_, tt_, state_a, state_b):
        proj, gcol, grow = _norm_proj(rows, norm_mix[0], w_main, tm=tm, tn=512, gates=gates)
        y_a, *fin_a = _mlstm(proj, gcol, grow, mlstm_norm[0], state_a, nb=nb, chunk=chunk_, dh=dh)
        y_b, *fin_b = _rglru(proj, 4, 5, lru_conv_w[0], lru_conv_b[0], w_ri, b_ri, lru_lambda[0], state_b,
                             nb=nb, tt=tt_)
        h = _mix_mlp(rows, [y_a, y_b], wo_ab, norm_mlp[0], w1[0], w2[0], tm=tm, tf=tf)
        return h, fin_a, fin_b

    zero_a = (jnp.zeros((A_HEADS, dh, dh + LANES), F32), jnp.zeros((A_HEADS, 1, LANES), F32))
    zero_b = (jnp.zeros((1, b_width), F32), jnp.zeros((SUBLANES, b_width), F32))
    h_meta, fin_a, fin_b = layer0(meta, 1, N_META, N_META, N_META, N_META, zero_a, zero_b)
    h_real, _, _ = layer0(xf, bsz, seq, tm_proj, chunk, tt, tuple(a[0] for a in fin_a),
                          tuple(a[0] for a in fin_b))

    lambda_init = 0.8 - 0.6 * math.exp(-0.3 * 1)
    w_qkv = c_w_in[0].astype(BF16)
    pos = jnp.arange(N_META + seq, dtype=jnp.int32)
    rope_meta = _rope_tables(pos[:N_META], dqk, rot_dim)
    rope_real = _rope_tables(pos[N_META:], dqk, rot_dim)
    qkv_meta = _norm_proj(h_meta, norm_mix[1], w_qkv, tm=N_META, tn=512, rope=rope_meta, rope_cols=d)
    def meta_block(a):
        a = a.reshape(N_META, c_heads, dv).transpose(1, 0, 2)
        return jnp.pad(a, ((0, 0), (0, LANES - N_META), (0, 0)))

    k_meta = meta_block(qkv_meta[:, d:2 * d])
    v_meta = meta_block(qkv_meta[:, 2 * d:])
    qkv = _norm_proj(h_real, norm_mix[1], w_qkv, tm=_pick_tile(seq, 512), tn=512, rope=rope_real, rope_cols=d)
    o = _diff_attn(qkv, k_meta, v_meta, c_lambda[0].astype(F32), c_subln[0].astype(F32), nb=bsz, seq=seq,
                   heads=c_heads, tq=tq, hp=2, lambda_init=lambda_init)
    out = _mix_mlp(h_real, [o], c_w_out[0].astype(BF16), norm_mlp[1], w1[1], w2[1], norm_final,
                   tm=tm_mlp, tf=tf)
    return out.reshape(bsz, seq, d)
```

```python
import functools
import math

import jax
import jax.numpy as jnp
from jax import lax
from jax.experimental import pallas as pl
from jax.experimental.pallas import tpu as pltpu

F32 = jnp.float32
BF16 = jnp.bfloat16

N_META = 16
EPS = 1e-6
A_HEADS = 4
B_BLOCKS = 8
B_CONV = 4
LRU_C = 8.0
ROPE_THETA = 500000.0

LANES = 128
SUBLANES = 8
GATE_ROWS = 16
NEG = -1e30
VMEM_LIMIT_BYTES = 56 * 1024 * 1024

_NT = (((1,), (1,)), ((), ()))
_TN = (((0,), (0,)), ((), ()))


def _cparams(*sem):
    return pltpu.CompilerParams(dimension_semantics=sem, vmem_limit_bytes=VMEM_LIMIT_BYTES)


def _sigmoid(x):
    return 0.5 * jnp.tanh(0.5 * x) + 0.5


def _sqrt_nonneg(x):
    return jnp.where(x > 0.0, x * lax.rsqrt(x), 0.0)


def _log_sigmoid(x):
    return jnp.minimum(x, 0.0) - jnp.log1p(jnp.exp(-jnp.abs(x)))


def _softplus(x):
    return jnp.maximum(x, 0.0) + jnp.log1p(jnp.exp(-jnp.abs(x)))


def _rms_scale(x):
    return x * lax.rsqrt(jnp.mean(x * x, axis=-1, keepdims=True) + EPS)


def _pick_tile(n, pref):
    t = min(n, pref)
    while n % t:
        t //= 2
    return t


def _resident(shape):
    return pl.BlockSpec(shape, lambda *_: (0,) * len(shape), pipeline_mode=pl.Buffered(1))


def _norm_proj_body(*refs, has_rope, has_gates, tn, rope_cols):
    it = iter(refs)
    x_ref, g_ref, w_ref = next(it), next(it), next(it)
    if has_rope:
        cos_ref, sa_ref, sb_ref = next(it), next(it), next(it)
    if has_gates:
        wgc_ref, wgr_ref, bc_ref, br_ref = next(it), next(it), next(it), next(it)
    o_ref = next(it)
    if has_gates:
        gcol_ref, grow_ref = next(it), next(it)
    xn_ref = next(it)

    xn = (_rms_scale(x_ref[...]) * g_ref[...]).astype(BF16)
    xn_ref[...] = xn
    if has_gates:
        gc = jnp.dot(xn, wgc_ref[...], preferred_element_type=F32) + bc_ref[...]
        lane = lax.broadcasted_iota(jnp.int32, gc.shape, 1)
        gcol_ref[...] = jnp.where(lane < A_HEADS, gc, _log_sigmoid(gc))
        gr = lax.dot_general(wgr_ref[...], xn, _NT, preferred_element_type=F32) + br_ref[...]
        sub = lax.broadcasted_iota(jnp.int32, gr.shape, 0)
        grow_ref[...] = jnp.where(sub < A_HEADS, gr, _log_sigmoid(gr))

    for c in range(o_ref.shape[1] // tn):
        acc = jnp.dot(xn_ref[...], w_ref[:, c * tn:(c + 1) * tn], preferred_element_type=F32)
        sec = (c * tn) // rope_cols if has_rope else None
        if has_rope and sec < cos_ref.shape[0]:
            cos, sa, sb = cos_ref[sec], sa_ref[sec], sb_ref[sec]
            for cc in range(tn // LANES):
                y = acc[:, cc * LANES:(cc + 1) * LANES]
                y = y * cos + pltpu.roll(y, LANES - 8, 1) * sa + pltpu.roll(y, 8, 1) * sb
                o_ref[:, c * tn + cc * LANES:c * tn + (cc + 1) * LANES] = y.astype(o_ref.dtype)
        else:
            o_ref[:, c * tn:(c + 1) * tn] = acc.astype(o_ref.dtype)


def _norm_proj(x, g, w, *, tm, tn, rope=None, rope_cols=None, gates=None):
    n, d = x.shape
    nout = w.shape[1]
    period = rope[0].shape[1] if rope is not None else n
    p_tiles = period // tm
    grid = (p_tiles, n // period)

    def rows(width):
        return pl.BlockSpec((tm, width), lambda p, r: (r * p_tiles + p, 0))

    in_specs = [rows(d), _resident((1, d)), _resident(w.shape)]
    args = [x, g.reshape(1, d), w]
    if rope is not None:
        spec = pl.BlockSpec((rope[0].shape[0], tm, LANES), lambda p, r: (0, p, 0))
        in_specs += [spec, spec, spec]
        args += list(rope)
    out_shape = [jax.ShapeDtypeStruct((n, nout), BF16)]
    out_specs = [rows(nout)]
    if gates is not None:
        in_specs += [_resident(a.shape) for a in gates]
        args += list(gates)
        out_shape += [jax.ShapeDtypeStruct((n, LANES), F32), jax.ShapeDtypeStruct((GATE_ROWS, n), F32)]
        out_specs += [rows(LANES), pl.BlockSpec((GATE_ROWS, tm), lambda p, r: (0, r * p_tiles + p))]
    body = functools.partial(_norm_proj_body, has_rope=rope is not None, has_gates=gates is not None, tn=tn,
                             rope_cols=rope_cols)
    outs = pl.pallas_call(
        body, grid=grid, in_specs=in_specs, out_specs=out_specs, out_shape=out_shape,
        scratch_shapes=[pltpu.VMEM((tm, d), BF16)],
        compiler_params=_cparams("parallel", "parallel"), name="norm_proj")(*args)
    return outs if gates is not None else outs[0]


def _mlstm_body(q_ref, k_ref, v_ref, o_ref, gcol_ref, grow_ref, nw_ref, c0_ref, m0_ref,
                y_ref, c_out, m_out, c_s, m_s, *, chunk, dh):
    ci = pl.program_id(1)

    @pl.when(ci == 0)
    def _():
        c_s[...] = c0_ref[...]
        m_s[...] = m0_ref[...]

    row = lax.broadcasted_iota(jnp.int32, (chunk, chunk), 0)
    col = lax.broadcasted_iota(jnp.int32, (chunk, chunk), 1)
    causal = col <= row
    bcol = jnp.dot(causal.astype(F32), gcol_ref[...], precision=lax.Precision.HIGHEST,
                   preferred_element_type=F32)
    brow = jnp.dot(grow_ref[...], (row <= col).astype(F32), precision=lax.Precision.HIGHEST,
                   preferred_element_type=F32)
    eye = (lax.broadcasted_iota(jnp.int32, (dh, dh), 0) == lax.broadcasted_iota(jnp.int32, (dh, dh), 1)).astype(BF16)
    ones = jnp.ones((chunk, LANES), BF16)
    scale = dh ** -0.5
    heads = range(A_HEADS)
    sls = [slice(h * dh, (h + 1) * dh) for h in heads]
    q = [q_ref[:, sl] for sl in sls]
    k = [k_ref[:, sl] * scale for sl in sls]
    v_aug = [jnp.concatenate([v_ref[:, sl], ones], axis=1) for sl in sls]
    b_c = [bcol[:, A_HEADS + h:A_HEADS + h + 1] for h in heads]
    li_r = [grow_ref[h:h + 1, :] for h in heads]
    b_r = [brow[A_HEADS + h:A_HEADS + h + 1, :] for h in heads]
    b_end = [b[chunk - 1:chunk, :] for b in b_c]
    m_prev = [m_s[h][:, 0:1] for h in heads]
    c_t = [c_s[h] for h in heads]

    qk = [lax.dot_general(q[h], k[h], _NT, preferred_element_type=F32) for h in heads]
    k_t = [lax.dot_general(eye, k[h], _NT, preferred_element_type=F32) for h in heads]
    inter = [jnp.dot(q[h], c_t[h].astype(BF16), preferred_element_type=F32) for h in heads]
    logd = [jnp.where(causal, b_c[h] - b_r[h] + li_r[h], NEG) for h in heads]
    log_prev = [b_c[h] + m_prev[h] for h in heads]
    m_t = [jnp.maximum(log_prev[h], jnp.max(logd[h], axis=1, keepdims=True)) for h in heads]
    s = [qk[h] * jnp.exp(logd[h] - m_t[h]) for h in heads]
    intra = [jnp.dot(s[h].astype(BF16), v_aug[h], preferred_element_type=F32) for h in heads]
    m_new = [jnp.maximum(b_end[h] + m_prev[h], jnp.max(b_end[h] - b_r[h] + li_r[h], axis=1, keepdims=True))
             for h in heads]
    kw_t = [(k_t[h] * jnp.exp(b_end[h] - b_r[h] + li_r[h] - m_new[h])).astype(BF16) for h in heads]
    upd = [jnp.dot(kw_t[h], v_aug[h], preferred_element_type=F32) for h in heads]
    for h in heads:
        num = jnp.exp(log_prev[h] - m_t[h]) * inter[h] + intra[h]
        den = num[:, dh:dh + 1]
        hh = num[:, :dh] / jnp.maximum(jnp.abs(den), jnp.exp(-m_t[h]))
        ha = _sigmoid(o_ref[:, sls[h]].astype(F32)) * hh
        y_ref[:, sls[h]] = (_rms_scale(ha) * nw_ref[:, sls[h]]).astype(y_ref.dtype)
        c_s[h] = jnp.exp(b_end[h] + m_prev[h] - m_new[h]) * c_t[h] + upd[h]
        m_s[h] = jnp.broadcast_to(m_new[h], (1, LANES))

    @pl.when(ci == pl.num_programs(1) - 1)
    def _():
        c_out[...] = c_s[...]
        m_out[...] = m_s[...]


def _mlstm(proj, gcol, grow, norm_w, state, *, nb, chunk, dh):
    n = gcol.shape[0]
    nc = n // nb // chunk
    width = A_HEADS * dh

    def tok(cb):
        return pl.BlockSpec((chunk, width), lambda b, c: (b * nc + c, cb))

    def whole(a):
        return pl.BlockSpec(a.shape, lambda b, c: (0,) * a.ndim)

    def per_seq(a):
        return pl.BlockSpec((None,) + a.shape, lambda b, c: (b,) + (0,) * a.ndim)

    nw = norm_w.reshape(1, width)
    in_specs = [tok(0), tok(1), tok(2), tok(3),
                pl.BlockSpec((chunk, LANES), lambda b, c: (b * nc + c, 0)),
                pl.BlockSpec((GATE_ROWS, chunk), lambda b, c: (0, b * nc + c)),
                whole(nw)] + [whole(a) for a in state]
    out_shape = [jax.ShapeDtypeStruct((n, width), BF16)] + [
        jax.ShapeDtypeStruct((nb,) + a.shape, F32) for a in state]
    out_specs = [pl.BlockSpec((chunk, width), lambda b, c: (b * nc + c, 0))] + [per_seq(a) for a in state]
    return pl.pallas_call(
        functools.partial(_mlstm_body, chunk=chunk, dh=dh),
        grid=(nb, nc), in_specs=in_specs, out_specs=out_specs, out_shape=out_shape,
        scratch_shapes=[pltpu.VMEM(a.shape, F32) for a in state],
        compiler_params=_cparams("parallel", "arbitrary"), name="mlstm",
    )(proj, proj, proj, proj, gcol, grow, nw, *state)


def _rglru_body(xb_ref, gate_ref, cw_ref, cb_ref, wri_ref, bri_ref, lam_ref, h0_ref, tail0_ref,
                y_ref, h_out, tail_out, h_s, xpad_s, *, tt, bd):
    @pl.when(pl.program_id(1) == 0)
    def _():
        h_s[...] = h0_ref[...]
        xpad_s[:SUBLANES, :] = tail0_ref[...]

    xpad_s[SUBLANES:, :] = xb_ref[...].astype(F32)
    row = lax.broadcasted_iota(jnp.int32, (tt, bd), 0)
    for n in range(B_BLOCKS):
        sl = slice(n * bd, (n + 1) * bd)
        xc = cb_ref[:, sl]
        for s in range(B_CONV):
            xc = xc + xpad_s[SUBLANES - s:SUBLANES - s + tt, sl] * cw_ref[B_CONV - 1 - s:B_CONV - s, sl]

        ri = jnp.dot(xc.astype(BF16), wri_ref[n], preferred_element_type=F32) + bri_ref[n]
        r = _sigmoid(ri[:, :bd])
        i = _sigmoid(ri[:, bd:])
        log_a = (-LRU_C * _softplus(-lam_ref[:, sl])) * r
        a = jnp.exp(log_a)
        u = _sqrt_nonneg(1.0 - a * a) * (i * xc)
        sh = 1
        while sh < tt:
            if sh < SUBLANES:
                keep = row >= sh
                u = jnp.where(keep, a * pltpu.roll(u, sh, 0) + u, u)
                a = jnp.where(keep, a * pltpu.roll(a, sh, 0), a)
            else:
                u = jnp.concatenate([u[:sh], a[sh:] * u[:tt - sh] + u[sh:]], axis=0)
                a = jnp.concatenate([a[:sh], a[sh:] * a[:tt - sh]], axis=0)
            sh *= 2
        hh = a * h_s[:, sl] + u
        h_s[:, sl] = hh[tt - 1:tt, :]
        y_ref[:, sl] = (hh * jax.nn.gelu(gate_ref[:, sl].astype(F32))).astype(y_ref.dtype)

    last8 = xpad_s[tt:, :]
    xpad_s[:SUBLANES, :] = last8

    @pl.when(pl.program_id(1) == pl.num_programs(1) - 1)
    def _():
        h_out[...] = h_s[...]
        tail_out[...] = last8


def _rglru(proj, xb_block, gate_block, conv_w, conv_b, w_ri, b_ri, lam, state, *, nb, tt):
    n = proj.shape[0]
    nt = n // nb // tt
    c = conv_w.shape[1]
    bd = c // B_BLOCKS
    h0, tail0 = state

    def whole(a):
        return pl.BlockSpec(a.shape, lambda b, t: (0,) * a.ndim)

    def per_seq(a):
        return pl.BlockSpec((None,) + a.shape, lambda b, t: (b,) + (0,) * a.ndim)

    consts = [conv_w, conv_b.reshape(1, c), w_ri, b_ri, lam.reshape(1, c), h0, tail0]
    in_specs = [pl.BlockSpec((tt, c), lambda b, t: (b * nt + t, xb_block)),
                pl.BlockSpec((tt, c), lambda b, t: (b * nt + t, gate_block))] + [whole(a) for a in consts]
    out_shape = [jax.ShapeDtypeStruct((n, c), BF16)] + [jax.ShapeDtypeStruct((nb,) + a.shape, F32) for a in state]
    out_specs = [pl.BlockSpec((tt, c), lambda b, t: (b * nt + t, 0))] + [per_seq(a) for a in state]
    return pl.pallas_call(
        functools.partial(_rglru_body, tt=tt, bd=bd),
        grid=(nb, nt), in_specs=in_specs, out_specs=out_specs, out_shape=out_shape,
        scratch_shapes=[pltpu.VMEM(h0.shape, F32), pltpu.VMEM((SUBLANES + tt, c), F32)],
        compiler_params=_cparams("parallel", "arbitrary"), name="rglru",
    )(proj, proj, *consts)


def _mix_mlp_body(*refs, n_y, final_norm, tf):
    it = iter(refs)
    x_ref = next(it)
    y_refs = [next(it) for _ in range(n_y)]
    wo_ref, g_ref, w1_ref, w2_ref = next(it), next(it), next(it), next(it)
    gf_ref = next(it) if final_norm else None
    o_ref, hn_s, acc_s = next(it), next(it), next(it)

    if n_y == 1:
        y = y_refs[0][...]
    else:
        ycat_s = next(it)
        off = 0
        for y_ref in y_refs:
            ycat_s[:, off:off + y_ref.shape[1]] = y_ref[...]
            off += y_ref.shape[1]
        y = ycat_s[...]
    acc_s[...] = x_ref[...] + jnp.dot(y, wo_ref[...], preferred_element_type=F32)
    hn_s[...] = (_rms_scale(acc_s[...]) * g_ref[...]).astype(BF16)
    for c in range(w1_ref.shape[1] // tf):
        a = jnp.maximum(jnp.dot(hn_s[...], w1_ref[:, c * tf:(c + 1) * tf], preferred_element_type=F32), 0.0)
        acc_s[...] += jnp.dot((a * a).astype(BF16), w2_ref[c * tf:(c + 1) * tf, :], preferred_element_type=F32)
    out = acc_s[...]
    if final_norm:
        out = _rms_scale(out) * gf_ref[...]
    o_ref[...] = out


def _mix_mlp(x, ys, wo, g, w1, w2, g_final=None, *, tm, tf):
    n, d = x.shape

    def rows(width):
        return pl.BlockSpec((tm, width), lambda i: (i, 0))

    in_specs = [rows(d)] + [rows(y.shape[1]) for y in ys]
    in_specs += [_resident(wo.shape), _resident((1, d)), _resident(w1.shape), _resident(w2.shape)]
    args = [x, *ys, wo, g.reshape(1, d), w1, w2]
    if g_final is not None:
        in_specs.append(_resident((1, d)))
        args.append(g_final.reshape(1, d))
    scratch = [pltpu.VMEM((tm, d), BF16), pltpu.VMEM((tm, d), F32)]
    if len(ys) > 1:
        scratch.append(pltpu.VMEM((tm, wo.shape[0]), BF16))
    return pl.pallas_call(
        functools.partial(_mix_mlp_body, n_y=len(ys), final_norm=g_final is not None, tf=tf),
        grid=(n // tm,), in_specs=in_specs, out_specs=rows(d),
        out_shape=jax.ShapeDtypeStruct((n, d), F32), scratch_shapes=scratch,
        compiler_params=_cparams("parallel"), name="mix_mlp")(*args)


def _diff_attn_body(q_ref, qn_ref, k_ref, v_ref, km_ref, vm_ref, lv_ref, sub_ref, o_ref, s_scr, mb_scr, acc_scr,
                    qs_scr, *, tq, dqk, hp, lambda_init):
    i = pl.program_id(2)
    nq = pl.num_programs(2) - 1
    dv = 2 * dqk
    n_fold = tq // LANES
    lane_q = lax.broadcasted_iota(jnp.int32, (1, dv), 1)
    sel1 = (lane_q < dqk).astype(BF16)
    sel2 = (lane_q >= dqk).astype(BF16)
    meta_lane = lax.broadcasted_iota(jnp.int32, (2 * tq, LANES), 1) < N_META

    def head(ref, hh, rows=slice(None)):
        return ref[rows, hh * dv:(hh + 1) * dv]

    def stack_queries(ref):
        for hh in range(hp):
            q = head(ref, hh)
            qs_scr[hh] = jnp.concatenate([q * sel1, q * sel2], axis=0)

    def finish():
        lv = lv_ref[...]
        lam = (jnp.exp(jnp.sum(lv[0:1] * lv[1:2], axis=1, keepdims=True))
               - jnp.exp(jnp.sum(lv[2:3] * lv[3:4], axis=1, keepdims=True)) + lambda_init)
        for hh in range(hp):
            acc = acc_scr[hh]
            o = acc[:, :dv] / acc[:, dv:]
            od = o[:tq] - lam * o[tq:]
            od = _rms_scale(od) * sub_ref[...] * (1.0 - lambda_init)
            o_ref[:, hh * dv:(hh + 1) * dv] = od.astype(o_ref.dtype)

    def scores(hh, k):
        return lax.dot_general(qs_scr[hh], k, _NT, preferred_element_type=F32)

    def lane_fold(x, op):
        r = x[:, :LANES]
        for c in range(1, x.shape[1] // LANES):
            r = op(r, x[:, c * LANES:(c + 1) * LANES])
        return r

    def key_block(j, nblk=1):
        return pl.ds(pl.multiple_of(j * tq, tq), nblk * tq)

    def with_ones(v):
        return jnp.concatenate([v, jnp.ones((v.shape[0], LANES), v.dtype)], axis=1)

    def in_pairs(n_blocks, fn):
        def pair(t, carry):
            fn(2 * t, 2)
            return carry

        lax.fori_loop(0, n_blocks // 2, pair, 0)

        @pl.when(n_blocks % 2 == 1)
        def _():
            fn(n_blocks - 1, 1)

    def pass1(j, nblk):
        for hh in range(hp):
            s = scores(hh, head(k_ref, hh, key_block(j, nblk)))
            for b in range(nblk):
                s_scr[hh, j + b] = s[:, b * tq:(b + 1) * tq]
            mb_scr[hh] = jnp.maximum(mb_scr[hh], lane_fold(s, jnp.maximum))

    def pass2(j, nblk):
        for hh in range(hp):
            mb = mb_scr[hh]
            p = jnp.concatenate(
                [jnp.exp2(s_scr[hh, j + b, :, c * LANES:(c + 1) * LANES] - mb)
                 for b in range(nblk) for c in range(n_fold)], axis=1)
            acc_scr[hh] += jnp.dot(p.astype(BF16), with_ones(head(v_ref, hh, key_block(j, nblk))),
                                   preferred_element_type=F32)

    @pl.when(i == 0)
    def _():
        stack_queries(q_ref)
        acc_scr[...] = jnp.ones(acc_scr.shape, F32)

    @pl.when(i < nq)
    def _():
        mb_scr[...] = jnp.full(mb_scr.shape, NEG, F32)
        in_pairs(i, pass1)
        finish()
        row = lax.broadcasted_iota(jnp.int32, (2 * tq, tq), 0)
        col = lax.broadcasted_iota(jnp.int32, (2 * tq, tq), 1)
        visible = col <= jnp.where(row >= tq, row - tq, row)
        s_diag = [jnp.where(visible, scores(hh, head(k_ref, hh, key_block(i))), NEG) for hh in range(hp)]
        s_meta = [jnp.where(meta_lane, scores(hh, km_ref[hh]), NEG) for hh in range(hp)]
        for hh in range(hp):
            s_scr[hh, i] = s_diag[hh]
            m_lanes = jnp.maximum(jnp.maximum(mb_scr[hh], s_meta[hh]), lane_fold(s_diag[hh], jnp.maximum))
            mb = jnp.broadcast_to(jnp.max(m_lanes, axis=1, keepdims=True), m_lanes.shape)
            mb_scr[hh] = mb
            p_meta = jnp.exp2(s_meta[hh] - mb)
            acc_scr[hh] = jnp.dot(p_meta.astype(BF16), with_ones(vm_ref[hh]), preferred_element_type=F32)
        stack_queries(qn_ref)
        in_pairs(i + 1, pass2)

    @pl.when(i == nq)
    def _():
        finish()


def _diff_attn(qkv, k_meta, v_meta, lam_vecs, subln, *, nb, seq, heads, tq, hp, lambda_init):
    n = qkv.shape[0]
    dv = subln.shape[0]
    nq = seq // tq
    hg = heads // hp
    w = hp * dv
    body = functools.partial(_diff_attn_body, tq=tq, dqk=dv // 2, hp=hp, lambda_init=lambda_init)
    return pl.pallas_call(
        body, grid=(nb, hg, nq + 1),
        in_specs=[pl.BlockSpec((tq, w), lambda b, h, i: (b * nq + jnp.minimum(i, nq - 1), h)),
                  pl.BlockSpec((tq, w), lambda b, h, i: (b * nq + jnp.minimum(i + 1, nq - 1), h)),
                  pl.BlockSpec((seq, w), lambda b, h, i: (b, hg + h)),
                  pl.BlockSpec((seq, w), lambda b, h, i: (b, 2 * hg + h)),
                  pl.BlockSpec((hp, LANES, dv), lambda b, h, i: (h, 0, 0)),
                  pl.BlockSpec((hp, LANES, dv), lambda b, h, i: (h, 0, 0)),
                  pl.BlockSpec(lam_vecs.shape, lambda b, h, i: (0, 0)),
                  pl.BlockSpec((1, dv), lambda b, h, i: (0, 0))],
        out_specs=pl.BlockSpec((tq, w), lambda b, h, i: (b * nq + jnp.maximum(i - 1, 0), h)),
        out_shape=jax.ShapeDtypeStruct((n, heads * dv), BF16),
        scratch_shapes=[pltpu.VMEM((hp, nq, 2 * tq, tq), F32),
                        pltpu.VMEM((hp, 2 * tq, LANES), F32),
                        pltpu.VMEM((hp, 2 * tq, dv + LANES), F32),
                        pltpu.VMEM((hp, 2 * tq, dv), BF16)],
        compiler_params=_cparams("parallel", "parallel", "arbitrary"), name="diff_attn",
    )(qkv, qkv, qkv, qkv, k_meta, v_meta, lam_vecs, subln.reshape(1, dv))


def _rope_tables(positions, dqk, rot_dim):
    half = rot_dim // 2
    inv_freq = jnp.power(jnp.float32(ROPE_THETA), -jnp.arange(0, rot_dim, 2, dtype=F32) / rot_dim)
    ang = positions.astype(F32)[:, None] * inv_freq[None, :]
    c = jnp.arange(LANES) % dqk
    idx = c % half
    cos = jnp.where(c < rot_dim, jnp.cos(ang)[:, idx], 1.0)
    sin = jnp.sin(ang)[:, idx]
    sa = jnp.where(c < half, -sin, 0.0)
    sb = jnp.where((c >= half) & (c < rot_dim), sin, 0.0)
    qs = dqk ** -0.5 * math.log2(math.e)
    return jnp.stack([cos * qs, cos]), jnp.stack([sa * qs, sa]), jnp.stack([sb * qs, sb])


def kernel(x, meta_tokens, norm_mix, norm_mlp, norm_final, ab_w_in, ab_if_bias, mlstm_norm, lru_conv_w,
           lru_conv_b, lru_w_r, lru_b_r, lru_w_i, lru_b_i, lru_lambda, ab_w_out, c_w_in, c_lambda,
           c_subln, c_w_out, mlp_w1, mlp_w2):
    bsz, seq, d = x.shape
    n = bsz * seq
    assert norm_mix.shape[0] == 2 and meta_tokens.shape[0] == N_META
    a_width = mlstm_norm.shape[1]
    dh = a_width // A_HEADS
    b_width = lru_conv_w.shape[2]
    bd = b_width // B_BLOCKS
    assert a_width == b_width == d
    c_heads = d // LANES
    dv = c_subln.shape[1]
    dqk = dv // 2
    rot_dim = dqk // 4

    tm_proj = _pick_tile(n, 512)
    chunk = _pick_tile(seq, 256)
    tt = _pick_tile(seq, 256)
    tm_mlp = _pick_tile(n, 512)
    tf = 1024
    tq = _pick_tile(seq, 512)

    xf = x.reshape(n, d)
    meta = meta_tokens.astype(x.dtype)

    w_in = ab_w_in[0]
    n_qkvo = 4 * a_width
    w_main = jnp.concatenate([w_in[:, :n_qkvo], w_in[:, n_qkvo + 2 * A_HEADS:]], axis=1).astype(BF16)
    w_g = w_in[:, n_qkvo:n_qkvo + 2 * A_HEADS]
    wg_col = jnp.pad(w_g, ((0, 0), (0, LANES - 2 * A_HEADS))).astype(BF16)
    wg_row = jnp.pad(w_g.T, ((0, GATE_ROWS - 2 * A_HEADS), (0, 0))).astype(BF16)
    ifb = ab_if_bias[0].astype(F32)
    gates = (wg_col, wg_row, jnp.pad(ifb, (0, LANES - 2 * A_HEADS)).reshape(1, LANES),
             jnp.pad(ifb, (0, GATE_ROWS - 2 * A_HEADS)).reshape(GATE_ROWS, 1))
    w_ri = jnp.concatenate([lru_w_r[0], lru_w_i[0]], axis=-1).astype(BF16)
    b_ri = jnp.concatenate([lru_b_r[0].reshape(B_BLOCKS, 1, bd), lru_b_i[0].reshape(B_BLOCKS, 1, bd)],
                           axis=-1).astype(F32)
    wo_ab = ab_w_out[0].astype(BF16)
    w1 = [mlp_w1[l].astype(BF16) for l in range(2)]
    w2 = [mlp_w2[l].astype(BF16) for l in range(2)]

    def layer0(rows, nb, t_rows, tm, chunk_, tt_, state_a, state_b):
        proj, gcol, grow = _norm_proj(rows, norm_mix[0], w_main, tm=tm, tn=512, gates=gates)
        y_a, *fin_a = _mlstm(proj, gcol, grow, mlstm_norm[0], state_a, nb=nb, chunk=chunk_, dh=dh)
        y_b, *fin_b = _rglru(proj, 4, 5, lru_conv_w[0], lru_conv_b[0], w_ri, b_ri, lru_lambda[0], state_b,
                             nb=nb, tt=tt_)
        h = _mix_mlp(rows, [y_a, y_b], wo_ab, norm_mlp[0], w1[0], w2[0], tm=tm, tf=tf)
        return h, fin_a, fin_b

    zero_a = (jnp.zeros((A_HEADS, dh, dh + LANES), F32), jnp.zeros((A_HEADS, 1, LANES), F32))
    zero_b = (jnp.zeros((1, b_width), F32), jnp.zeros((SUBLANES, b_width), F32))
    h_meta, fin_a, fin_b = layer0(meta, 1, N_META, N_META, N_META, N_META, zero_a, zero_b)
    h_real, _, _ = layer0(xf, bsz, seq, tm_proj, chunk, tt, tuple(a[0] for a in fin_a),
                          tuple(a[0] for a in fin_b))

    lambda_init = 0.8 - 0.6 * math.exp(-0.3 * 1)
    w_qkv = c_w_in[0].astype(BF16)
    pos = jnp.arange(N_META + seq, dtype=jnp.int32)
    rope_meta = _rope_tables(pos[:N_META], dqk, rot_dim)
    rope_real = _rope_tables(pos[N_META:], dqk, rot_dim)
    qkv_meta = _norm_proj(h_meta, norm_mix[1], w_qkv, tm=N_META, tn=512, rope=rope_meta, rope_cols=d)
    def meta_block(a):
        a = a.reshape(N_META, c_heads, dv).transpose(1, 0, 2)
        return jnp.pad(a, ((0, 0), (0, LANES - N_META), (0, 0)))

    k_meta = meta_block(qkv_meta[:, d:2 * d])
    v_meta = meta_block(qkv_meta[:, 2 * d:])
    qkv = _norm_proj(h_real, norm_mix[1], w_qkv, tm=_pick_tile(seq, 512), tn=512, rope=rope_real, rope_cols=d)
    o = _diff_attn(qkv, k_meta, v_meta, c_lambda[0].astype(F32), c_subln[0].astype(F32), nb=bsz, seq=seq,
                   heads=c_heads, tq=tq, hp=2, lambda_init=lambda_init)
    out = _mix_mlp(h_real, [o], c_w_out[0].astype(BF16), norm_mlp[1], w1[1], w2[1], norm_final,
                   tm=tm_mlp, tf=tf)
    return out.reshape(bsz, seq, d)
```

```python
import functools
import math

import jax
import jax.numpy as jnp
from jax import lax
from jax.experimental import pallas as pl
from jax.experimental.pallas import tpu as pltpu

F32 = jnp.float32
BF16 = jnp.bfloat16

N_META = 16
EPS = 1e-6
A_HEADS = 4
B_BLOCKS = 8
B_CONV = 4
LRU_C = 8.0
ROPE_THETA = 500000.0

LANES = 128
SUBLANES = 8
GATE_ROWS = 16
NEG = -1e30
VMEM_LIMIT_BYTES = 56 * 1024 * 1024

_NT = (((1,), (1,)), ((), ()))
_TN = (((0,), (0,)), ((), ()))


def _cparams(*sem):
    return pltpu.CompilerParams(dimension_semantics=sem, vmem_limit_bytes=VMEM_LIMIT_BYTES)


def _sigmoid(x):
    return 0.5 * jnp.tanh(0.5 * x) + 0.5


def _sqrt_nonneg(x):
    return jnp.where(x > 0.0, x * lax.rsqrt(x), 0.0)


def _log_sigmoid(x):
    return jnp.minimum(x, 0.0) - jnp.log1p(jnp.exp(-jnp.abs(x)))


def _softplus(x):
    return jnp.maximum(x, 0.0) + jnp.log1p(jnp.exp(-jnp.abs(x)))


def _rms_scale(x):
    return x * lax.rsqrt(jnp.mean(x * x, axis=-1, keepdims=True) + EPS)


def _pick_tile(n, pref):
    t = min(n, pref)
    while n % t:
        t //= 2
    return t


def _resident(shape):
    return pl.BlockSpec(shape, lambda *_: (0,) * len(shape), pipeline_mode=pl.Buffered(1))


def _norm_proj_body(*refs, has_rope, has_gates, tn, rope_cols):
    it = iter(refs)
    x_ref, g_ref, w_ref = next(it), next(it), next(it)
    if has_rope:
        cos_ref, sa_ref, sb_ref = next(it), next(it), next(it)
    if has_gates:
        wgc_ref, wgr_ref, bc_ref, br_ref = next(it), next(it), next(it), next(it)
    o_ref = next(it)
    if has_gates:
        gcol_ref, grow_ref = next(it), next(it)
    xn_ref = next(it)

    xn = (_rms_scale(x_ref[...]) * g_ref[...]).astype(BF16)
    xn_ref[...] = xn
    if has_gates:
        gc = jnp.dot(xn, wgc_ref[...], preferred_element_type=F32) + bc_ref[...]
        lane = lax.broadcasted_iota(jnp.int32, gc.shape, 1)
        gcol_ref[...] = jnp.where(lane < A_HEADS, gc, _log_sigmoid(gc))
        gr = lax.dot_general(wgr_ref[...], xn, _NT, preferred_element_type=F32) + br_ref[...]
        sub = lax.broadcasted_iota(jnp.int32, gr.shape, 0)
        grow_ref[...] = jnp.where(sub < A_HEADS, gr, _log_sigmoid(gr))

    for c in range(o_ref.shape[1] // tn):
        acc = jnp.dot(xn_ref[...], w_ref[:, c * tn:(c + 1) * tn], preferred_element_type=F32)
        sec = (c * tn) // rope_cols if has_rope else None
        if has_rope and sec < cos_ref.shape[0]:
            cos, sa, sb = cos_ref[sec], sa_ref[sec], sb_ref[sec]
            for cc in range(tn // LANES):
                y = acc[:, cc * LANES:(cc + 1) * LANES]
                y = y * cos + pltpu.roll(y, LANES - 8, 1) * sa + pltpu.roll(y, 8, 1) * sb
                o_ref[:, c * tn + cc * LANES:c * tn + (cc + 1) * LANES] = y.astype(o_ref.dtype)
        else:
            o_ref[:, c * tn:(c + 1) * tn] = acc.astype(o_ref.dtype)


def _norm_proj(x, g, w, *, tm, tn, rope=None, rope_cols=None, gates=None):
    n, d = x.shape
    nout = w.shape[1]
    period = rope[0].shape[1] if rope is not None else n
    p_tiles = period // tm
    grid = (p_tiles, n // period)

    def rows(width):
        return pl.BlockSpec((tm, width), lambda p, r: (r * p_tiles + p, 0))

    in_specs = [rows(d), _resident((1, d)), _resident(w.shape)]
    args = [x, g.reshape(1, d), w]
    if rope is not None:
        spec = pl.BlockSpec((rope[0].shape[0], tm, LANES), lambda p, r: (0, p, 0))
        in_specs += [spec, spec, spec]
        args += list(rope)
    out_shape = [jax.ShapeDtypeStruct((n, nout), BF16)]
    out_specs = [rows(nout)]
    if gates is not None:
        in_specs += [_resident(a.shape) for a in gates]
        args += list(gates)
        out_shape += [jax.ShapeDtypeStruct((n, LANES), F32), jax.ShapeDtypeStruct((GATE_ROWS, n), F32)]
        out_specs += [rows(LANES), pl.BlockSpec((GATE_ROWS, tm), lambda p, r: (0, r * p_tiles + p))]
    body = functools.partial(_norm_proj_body, has_rope=rope is not None, has_gates=gates is not None, tn=tn,
                             rope_cols=rope_cols)
    outs = pl.pallas_call(
        body, grid=grid, in_specs=in_specs, out_specs=out_specs, out_shape=out_shape,
        scratch_shapes=[pltpu.VMEM((tm, d), BF16)],
        compiler_params=_cparams("parallel", "parallel"), name="norm_proj")(*args)
    return outs if gates is not None else outs[0]


def _mlstm_body(q_ref, k_ref, v_ref, o_ref, gcol_ref, grow_ref, nw_ref, c0_ref, m0_ref,
                y_ref, c_out, m_out, c_s, m_s, *, chunk, dh):
    ci = pl.program_id(1)

    @pl.when(ci == 0)
    def _():
        c_s[...] = c0_ref[...]
        m_s[...] = m0_ref[...]

    row = lax.broadcasted_iota(jnp.int32, (chunk, chunk), 0)
    col = lax.broadcasted_iota(jnp.int32, (chunk, chunk), 1)
    causal = col <= row
    bcol = jnp.dot(causal.astype(F32), gcol_ref[...], precision=lax.Precision.HIGHEST,
                   preferred_element_type=F32)
    brow = jnp.dot(grow_ref[...], (row <= col).astype(F32), precision=lax.Precision.HIGHEST,
                   preferred_element_type=F32)
    eye = (lax.broadcasted_iota(jnp.int32, (dh, dh), 0) == lax.broadcasted_iota(jnp.int32, (dh, dh), 1)).astype(BF16)
    ones = jnp.ones((chunk, LANES), BF16)
    scale = dh ** -0.5
    heads = range(A_HEADS)
    sls = [slice(h * dh, (h + 1) * dh) for h in heads]
    q = [q_ref[:, sl] for sl in sls]
    k = [k_ref[:, sl] * scale for sl in sls]
    v_aug = [jnp.concatenate([v_ref[:, sl], ones], axis=1) for sl in sls]
    b_c = [bcol[:, A_HEADS + h:A_HEADS + h + 1] for h in heads]
    li_r = [grow_ref[h:h + 1, :] for h in heads]
    b_r = [brow[A_HEADS + h:A_HEADS + h + 1, :] for h in heads]
    b_end = [b[chunk - 1:chunk, :] for b in b_c]
    m_prev = [m_s[h][:, 0:1] for h in heads]
    c_t = [c_s[h] for h in heads]

    qk = [lax.dot_general(q[h], k[h], _NT, preferred_element_type=F32) for h in heads]
    k_t = [lax.dot_general(eye, k[h], _NT, preferred_element_type=F32) for h in heads]
    inter = [jnp.dot(q[h], c_t[h].astype(BF16), preferred_element_type=F32) for h in heads]
    logd = [jnp.where(causal, b_c[h] - b_r[h] + li_r[h], NEG) for h in heads]
    log_prev = [b_c[h] + m_prev[h] for h in heads]
    m_t = [jnp.maximum(log_prev[h], jnp.max(logd[h], axis=1, keepdims=True)) for h in heads]
    s = [qk[h] * jnp.exp(logd[h] - m_t[h]) for h in heads]
    intra = [jnp.dot(s[h].astype(BF16), v_aug[h], preferred_element_type=F32) for h in heads]
    m_new = [jnp.maximum(b_end[h] + m_prev[h], jnp.max(b_end[h] - b_r[h] + li_r[h], axis=1, keepdims=True))
             for h in heads]
    kw_t = [(k_t[h] * jnp.exp(b_end[h] - b_r[h] + li_r[h] - m_new[h])).astype(BF16) for h in heads]
    upd = [jnp.dot(kw_t[h], v_aug[h], preferred_element_type=F32) for h in heads]
    for h in heads:
        num = jnp.exp(log_prev[h] - m_t[h]) * inter[h] + intra[h]
        den = num[:, dh:dh + 1]
        hh = num[:, :dh] / jnp.maximum(jnp.abs(den), jnp.exp(-m_t[h]))
        ha = _sigmoid(o_ref[:, sls[h]].astype(F32)) * hh
        y_ref[:, sls[h]] = (_rms_scale(ha) * nw_ref[:, sls[h]]).astype(y_ref.dtype)
        c_s[h] = jnp.exp(b_end[h] + m_prev[h] - m_new[h]) * c_t[h] + upd[h]
        m_s[h] = jnp.broadcast_to(m_new[h], (1, LANES))

    @pl.when(ci == pl.num_programs(1) - 1)
    def _():
        c_out[...] = c_s[...]
        m_out[...] = m_s[...]


def _mlstm(proj, gcol, grow, norm_w, state, *, nb, chunk, dh):
    n = gcol.shape[0]
    nc = n // nb // chunk
    width = A_HEADS * dh

    def tok(cb):
        return pl.BlockSpec((chunk, width), lambda b, c: (b * nc + c, cb))

    def whole(a):
        return pl.BlockSpec(a.shape, lambda b, c: (0,) * a.ndim)

    def per_seq(a):
        return pl.BlockSpec((None,) + a.shape, lambda b, c: (b,) + (0,) * a.ndim)

    nw = norm_w.reshape(1, width)
    in_specs = [tok(0), tok(1), tok(2), tok(3),
                pl.BlockSpec((chunk, LANES), lambda b, c: (b * nc + c, 0)),
                pl.BlockSpec((GATE_ROWS, chunk), lambda b, c: (0, b * nc + c)),
                whole(nw)] + [whole(a) for a in state]
    out_shape = [jax.ShapeDtypeStruct((n, width), BF16)] + [
        jax.ShapeDtypeStruct((nb,) + a.shape, F32) for a in state]
    out_specs = [pl.BlockSpec((chunk, width), lambda b, c: (b * nc + c, 0))] + [per_seq(a) for a in state]
    return pl.pallas_call(
        functools.partial(_mlstm_body, chunk=chunk, dh=dh),
        grid=(nb, nc), in_specs=in_specs, out_specs=out_specs, out_shape=out_shape,
        scratch_shapes=[pltpu.VMEM(a.shape, F32) for a in state],
        compiler_params=_cparams("parallel", "arbitrary"), name="mlstm",
    )(proj, proj, proj, proj, gcol, grow, nw, *state)


def _rglru_body(xb_ref, gate_ref, cw_ref, cb_ref, wri_ref, bri_ref, lam_ref, h0_ref, tail0_ref,
                y_ref, h_out, tail_out, h_s, xpad_s, *, tt, bd):
    @pl.when(pl.program_id(1) == 0)
    def _():
        h_s[...] = h0_ref[...]
        xpad_s[:SUBLANES, :] = tail0_ref[...]

    xpad_s[SUBLANES:, :] = xb_ref[...].astype(F32)
    row = lax.broadcasted_iota(jnp.int32, (tt, bd), 0)
    grp = min(tt, 4 * SUBLANES)
    for n in range(B_BLOCKS):
        sl = slice(n * bd, (n + 1) * bd)
        xc = cb_ref[:, sl]
        for s in range(B_CONV):
            xc = xc + xpad_s[SUBLANES - s:SUBLANES - s + tt, sl] * cw_ref[B_CONV - 1 - s:B_CONV - s, sl]

        ri = jnp.dot(xc.astype(BF16), wri_ref[n], preferred_element_type=F32) + bri_ref[n]
        r = _sigmoid(ri[:, :bd])
        i = _sigmoid(ri[:, bd:])
        log_a = (-LRU_C * _softplus(-lam_ref[:, sl])) * r
        a = jnp.exp(log_a)
        u = _sqrt_nonneg(1.0 - a * a) * (i * xc)
        sh = 1
        while sh < grp:
            if sh < SUBLANES:
                keep = row % grp >= sh
                u = jnp.where(keep, a * pltpu.roll(u, sh, 0) + u, u)
                a = jnp.where(keep, a * pltpu.roll(a, sh, 0), a)
            else:
                parts_u, parts_a = [], []
                for g0 in range(0, tt, grp):
                    ag, ug = a[g0:g0 + grp], u[g0:g0 + grp]
                    parts_u += [ug[:sh], ag[sh:] * ug[:grp - sh] + ug[sh:]]
                    parts_a += [ag[:sh], ag[sh:] * ag[:grp - sh]]
                u, a = jnp.concatenate(parts_u, axis=0), jnp.concatenate(parts_a, axis=0)
            sh *= 2
        carry = h_s[:, sl]
        pieces = []
        for g0 in range(0, tt, grp):
            hg = a[g0:g0 + grp] * carry + u[g0:g0 + grp]
            carry = hg[grp - 1:grp, :]
            pieces.append(hg)
        hh = jnp.concatenate(pieces, axis=0)
        h_s[:, sl] = carry
        y_ref[:, sl] = (hh * jax.nn.gelu(gate_ref[:, sl].astype(F32))).astype(y_ref.dtype)

    last8 = xpad_s[tt:, :]
    xpad_s[:SUBLANES, :] = last8

    @pl.when(pl.program_id(1) == pl.num_programs(1) - 1)
    def _():
        h_out[...] = h_s[...]
        tail_out[...] = last8


def _rglru(proj, xb_block, gate_block, conv_w, conv_b, w_ri, b_ri, lam, state, *, nb, tt):
    n = proj.shape[0]
    nt = n // nb // tt
    c = conv_w.shape[1]
    bd = c // B_BLOCKS
    h0, tail0 = state

    def whole(a):
        return pl.BlockSpec(a.shape, lambda b, t: (0,) * a.ndim)

    def per_seq(a):
        return pl.BlockSpec((None,) + a.shape, lambda b, t: (b,) + (0,) * a.ndim)

    consts = [conv_w, conv_b.reshape(1, c), w_ri, b_ri, lam.reshape(1, c), h0, tail0]
    in_specs = [pl.BlockSpec((tt, c), lambda b, t: (b * nt + t, xb_block)),
                pl.BlockSpec((tt, c), lambda b, t: (b * nt + t, gate_block))] + [whole(a) for a in consts]
    out_shape = [jax.ShapeDtypeStruct((n, c), BF16)] + [jax.ShapeDtypeStruct((nb,) + a.shape, F32) for a in state]
    out_specs = [pl.BlockSpec((tt, c), lambda b, t: (b * nt + t, 0))] + [per_seq(a) for a in state]
    return pl.pallas_call(
        functools.partial(_rglru_body, tt=tt, bd=bd),
        grid=(nb, nt), in_specs=in_specs, out_specs=out_specs, out_shape=out_shape,
        scratch_shapes=[pltpu.VMEM(h0.shape, F32), pltpu.VMEM((SUBLANES + tt, c), F32)],
        compiler_params=_cparams("parallel", "arbitrary"), name="rglru",
    )(proj, proj, *consts)


def _mix_mlp_body(*refs, n_y, final_norm, tf):
    it = iter(refs)
    x_ref = next(it)
    y_refs = [next(it) for _ in range(n_y)]
    wo_ref, g_ref, w1_ref, w2_ref = next(it), next(it), next(it), next(it)
    gf_ref = next(it) if final_norm else None
    o_ref, hn_s, acc_s = next(it), next(it), next(it)

    if n_y == 1:
        y = y_refs[0][...]
    else:
        ycat_s = next(it)
        off = 0
        for y_ref in y_refs:
            ycat_s[:, off:off + y_ref.shape[1]] = y_ref[...]
            off += y_ref.shape[1]
        y = ycat_s[...]
    acc_s[...] = x_ref[...] + jnp.dot(y, wo_ref[...], preferred_element_type=F32)
    hn_s[...] = (_rms_scale(acc_s[...]) * g_ref[...]).astype(BF16)
    for c in range(w1_ref.shape[1] // tf):
        a = jnp.maximum(jnp.dot(hn_s[...], w1_ref[:, c * tf:(c + 1) * tf], preferred_element_type=F32), 0.0)
        acc_s[...] += jnp.dot((a * a).astype(BF16), w2_ref[c * tf:(c + 1) * tf, :], preferred_element_type=F32)
    out = acc_s[...]
    if final_norm:
        out = _rms_scale(out) * gf_ref[...]
    o_ref[...] = out


def _mix_mlp(x, ys, wo, g, w1, w2, g_final=None, *, tm, tf):
    n, d = x.shape

    def rows(width):
        return pl.BlockSpec((tm, width), lambda i: (i, 0))

    in_specs = [rows(d)] + [rows(y.shape[1]) for y in ys]
    in_specs += [_resident(wo.shape), _resident((1, d)), _resident(w1.shape), _resident(w2.shape)]
    args = [x, *ys, wo, g.reshape(1, d), w1, w2]
    if g_final is not None:
        in_specs.append(_resident((1, d)))
        args.append(g_final.reshape(1, d))
    scratch = [pltpu.VMEM((tm, d), BF16), pltpu.VMEM((tm, d), F32)]
    if len(ys) > 1:
        scratch.append(pltpu.VMEM((tm, wo.shape[0]), BF16))
    return pl.pallas_call(
        functools.partial(_mix_mlp_body, n_y=len(ys), final_norm=g_final is not None, tf=tf),
        grid=(n // tm,), in_specs=in_specs, out_specs=rows(d),
        out_shape=jax.ShapeDtypeStruct((n, d), F32), scratch_shapes=scratch,
        compiler_params=_cparams("parallel"), name="mix_mlp")(*args)


def _diff_attn_body(q_ref, k_ref, v_ref, km_ref, vm_ref, lv_ref, sub_ref, o_ref, s_scr, mb_scr, acc_scr,
                    *, tq, dqk, hp, lambda_init):
    qi = pl.program_id(2)
    dv = 2 * dqk
    n_fold = tq // LANES
    lane_q = lax.broadcasted_iota(jnp.int32, (1, dv), 1)
    sel1 = (lane_q < dqk).astype(BF16)
    sel2 = (lane_q >= dqk).astype(BF16)
    meta_lane = lax.broadcasted_iota(jnp.int32, (2 * tq, LANES), 1) < N_META

    def head(ref, hh, rows=slice(None)):
        return ref[rows, hh * dv:(hh + 1) * dv]

    qs = []
    for hh in range(hp):
        q = head(q_ref, hh)
        qs.append(jnp.concatenate([q * sel1, q * sel2], axis=0))

    def scores(hh, k):
        return lax.dot_general(qs[hh], k, _NT, preferred_element_type=F32)

    def lane_fold(x, op):
        r = x[:, :LANES]
        for c in range(1, x.shape[1] // LANES):
            r = op(r, x[:, c * LANES:(c + 1) * LANES])
        return r

    def key_block(j, nblk=1):
        return pl.ds(pl.multiple_of(j * tq, tq), nblk * tq)

    def with_ones(v):
        return jnp.concatenate([v, jnp.ones((v.shape[0], LANES), v.dtype)], axis=1)

    mb_scr[...] = jnp.full(mb_scr.shape, NEG, F32)

    def pass1(j, nblk):
        for hh in range(hp):
            s = scores(hh, head(k_ref, hh, key_block(j, nblk)))
            for b in range(nblk):
                s_scr[hh, j + b] = s[:, b * tq:(b + 1) * tq]
            mb_scr[hh] = jnp.maximum(mb_scr[hh], lane_fold(s, jnp.maximum))

    def in_pairs(n_blocks, fn):
        def pair(t, carry):
            fn(2 * t, 2)
            return carry

        lax.fori_loop(0, n_blocks // 2, pair, 0)

        @pl.when(n_blocks % 2 == 1)
        def _():
            fn(n_blocks - 1, 1)

    in_pairs(qi, pass1)
    row = lax.broadcasted_iota(jnp.int32, (2 * tq, tq), 0)
    col = lax.broadcasted_iota(jnp.int32, (2 * tq, tq), 1)
    visible = col <= jnp.where(row >= tq, row - tq, row)
    s_diag = [jnp.where(visible, scores(hh, head(k_ref, hh, key_block(qi))), NEG) for hh in range(hp)]
    s_meta = [jnp.where(meta_lane, scores(hh, km_ref[hh]), NEG) for hh in range(hp)]
    for hh in range(hp):
        s_scr[hh, qi] = s_diag[hh]
        m_lanes = jnp.maximum(jnp.maximum(mb_scr[hh], s_meta[hh]), lane_fold(s_diag[hh], jnp.maximum))
        mb = jnp.broadcast_to(jnp.max(m_lanes, axis=1, keepdims=True), m_lanes.shape)
        mb_scr[hh] = mb
        p_meta = jnp.exp2(s_meta[hh] - mb)
        acc_scr[hh] = jnp.dot(p_meta.astype(BF16), with_ones(vm_ref[hh]), preferred_element_type=F32)

    def pass2(j, nblk):
        for hh in range(hp):
            mb = mb_scr[hh]
            p = jnp.concatenate(
                [jnp.exp2(s_scr[hh, j + b, :, c * LANES:(c + 1) * LANES] - mb)
                 for b in range(nblk) for c in range(n_fold)], axis=1)
            acc_scr[hh] += jnp.dot(p.astype(BF16), with_ones(head(v_ref, hh, key_block(j, nblk))),
                                   preferred_element_type=F32)

    in_pairs(qi + 1, pass2)
    lv = lv_ref[...]
    lam = (jnp.exp(jnp.sum(lv[0:1] * lv[1:2], axis=1, keepdims=True))
           - jnp.exp(jnp.sum(lv[2:3] * lv[3:4], axis=1, keepdims=True)) + lambda_init)
    for hh in range(hp):
        acc = acc_scr[hh]
        o = acc[:, :dv] / acc[:, dv:]
        od = o[:tq] - lam * o[tq:]
        od = _rms_scale(od) * sub_ref[...] * (1.0 - lambda_init)
        o_ref[:, hh * dv:(hh + 1) * dv] = od.astype(o_ref.dtype)


def _diff_attn(qkv, k_meta, v_meta, lam_vecs, subln, *, nb, seq, heads, tq, hp, lambda_init):
    n = qkv.shape[0]
    dv = subln.shape[0]
    nq = seq // tq
    hg = heads // hp
    w = hp * dv
    body = functools.partial(_diff_attn_body, tq=tq, dqk=dv // 2, hp=hp, lambda_init=lambda_init)
    stat = pltpu.VMEM((hp, 2 * tq, LANES), F32)
    acc = pltpu.VMEM((hp, 2 * tq, dv + LANES), F32)
    return pl.pallas_call(
        body, grid=(nb, hg, nq),
        in_specs=[pl.BlockSpec((tq, w), lambda b, h, i: (b * nq + i, h)),
                  pl.BlockSpec((seq, w), lambda b, h, i: (b, hg + h)),
                  pl.BlockSpec((seq, w), lambda b, h, i: (b, 2 * hg + h)),
                  pl.BlockSpec((hp, LANES, dv), lambda b, h, i: (h, 0, 0)),
                  pl.BlockSpec((hp, LANES, dv), lambda b, h, i: (h, 0, 0)),
                  pl.BlockSpec(lam_vecs.shape, lambda b, h, i: (0, 0)),
                  pl.BlockSpec((1, dv), lambda b, h, i: (0, 0))],
        out_specs=pl.BlockSpec((tq, w), lambda b, h, i: (b * nq + i, h)),
        out_shape=jax.ShapeDtypeStruct((n, heads * dv), BF16),
        scratch_shapes=[pltpu.VMEM((hp, nq, 2 * tq, tq), F32), stat, acc],
        compiler_params=_cparams("parallel", "parallel", "arbitrary"), name="diff_attn",
    )(qkv, qkv, qkv, k_meta, v_meta, lam_vecs, subln.reshape(1, dv))


def _rope_tables(positions, dqk, rot_dim):
    inv_freq = jnp.power(jnp.float32(ROPE_THETA), -jnp.arange(0, rot_dim, 2, dtype=F32) / rot_dim)
    ang = positions.astype(F32)[:, None] * inv_freq[None, :]
    cos_h, sin_h = jnp.cos(ang), jnp.sin(ang)
    rest = dqk - rot_dim

    def lanes(first, second, fill):
        comp = jnp.concatenate([first, second, jnp.full((ang.shape[0], rest), fill, F32)], axis=1)
        return jnp.concatenate([comp] * (LANES // dqk), axis=1)

    cos = lanes(cos_h, cos_h, 1.0)
    sa = lanes(-sin_h, jnp.zeros_like(sin_h), 0.0)
    sb = lanes(jnp.zeros_like(sin_h), sin_h, 0.0)
    qs = dqk ** -0.5 * math.log2(math.e)
    return jnp.stack([cos * qs, cos]), jnp.stack([sa * qs, sa]), jnp.stack([sb * qs, sb])


def kernel(x, meta_tokens, norm_mix, norm_mlp, norm_final, ab_w_in, ab_if_bias, mlstm_norm, lru_conv_w,
           lru_conv_b, lru_w_r, lru_b_r, lru_w_i, lru_b_i, lru_lambda, ab_w_out, c_w_in, c_lambda,
           c_subln, c_w_out, mlp_w1, mlp_w2):
    bsz, seq, d = x.shape
    n = bsz * seq
    assert norm_mix.shape[0] == 2 and meta_tokens.shape[0] == N_META
    a_width = mlstm_norm.shape[1]
    dh = a_width // A_HEADS
    b_width = lru_conv_w.shape[2]
    bd = b_width // B_BLOCKS
    assert a_width == b_width == d
    c_heads = d // LANES
    dv = c_subln.shape[1]
    dqk = dv // 2
    rot_dim = dqk // 4

    tm_proj = _pick_tile(n, 512)
    chunk = _pick_tile(seq, 256)
    tt = _pick_tile(seq, 512)
    tm_mlp = _pick_tile(n, 512)
    tf = 1024
    tq = _pick_tile(seq, 512)

    xf = x.reshape(n, d)
    meta = meta_tokens.astype(x.dtype)

    w_in = ab_w_in[0]
    n_qkvo = 4 * a_width
    w_main = jnp.concatenate([w_in[:, :n_qkvo], w_in[:, n_qkvo + 2 * A_HEADS:]], axis=1).astype(BF16)
    w_g = w_in[:, n_qkvo:n_qkvo + 2 * A_HEADS]
    wg_col = jnp.pad(w_g, ((0, 0), (0, LANES - 2 * A_HEADS))).astype(BF16)
    wg_row = jnp.pad(w_g.T, ((0, GATE_ROWS - 2 * A_HEADS), (0, 0))).astype(BF16)
    ifb = ab_if_bias[0].astype(F32)
    gates = (wg_col, wg_row, jnp.pad(ifb, (0, LANES - 2 * A_HEADS)).reshape(1, LANES),
             jnp.pad(ifb, (0, GATE_ROWS - 2 * A_HEADS)).reshape(GATE_ROWS, 1))
    w_ri = jnp.concatenate([lru_w_r[0], lru_w_i[0]], axis=-1).astype(BF16)
    b_ri = jnp.concatenate([lru_b_r[0].reshape(B_BLOCKS, 1, bd), lru_b_i[0].reshape(B_BLOCKS, 1, bd)],
                           axis=-1).astype(F32)
    wo_ab = ab_w_out[0].astype(BF16)
    w1 = [mlp_w1[l].astype(BF16) for l in range(2)]
    w2 = [mlp_w2[l].astype(BF16) for l in range(2)]

    def layer0(rows, nb, t_rows, tm, chunk_, tt_, state_a, state_b):
        proj, gcol, grow = _norm_proj(rows, norm_mix[0], w_main, tm=tm, tn=512, gates=gates)
        y_a, *fin_a = _mlstm(proj, gcol, grow, mlstm_norm[0], state_a, nb=nb, chunk=chunk_, dh=dh)
        y_b, *fin_b = _rglru(proj, 4, 5, lru_conv_w[0], lru_conv_b[0], w_ri, b_ri, lru_lambda[0], state_b,
                             nb=nb, tt=tt_)
        h = _mix_mlp(rows, [y_a, y_b], wo_ab, norm_mlp[0], w1[0], w2[0], tm=tm, tf=tf)
        return h, fin_a, fin_b

    zero_a = (jnp.zeros((A_HEADS, dh, dh + LANES), F32), jnp.zeros((A_HEADS, 1, LANES), F32))
    zero_b = (jnp.zeros((1, b_width), F32), jnp.zeros((SUBLANES, b_width), F32))
    h_meta, fin_a, fin_b = layer0(meta, 1, N_META, N_META, N_META, N_META, zero_a, zero_b)
    h_real, _, _ = layer0(xf, bsz, seq, tm_proj, chunk, tt, tuple(a[0] for a in fin_a),
                          tuple(a[0] for a in fin_b))

    lambda_init = 0.8 - 0.6 * math.exp(-0.3 * 1)
    w_qkv = c_w_in[0].astype(BF16)
    pos = jnp.arange(N_META + seq, dtype=jnp.int32)
    rope_meta = _rope_tables(pos[:N_META], dqk, rot_dim)
    rope_real = _rope_tables(pos[N_META:], dqk, rot_dim)
    qkv_meta = _norm_proj(h_meta, norm_mix[1], w_qkv, tm=N_META, tn=512, rope=rope_meta, rope_cols=d)
    def meta_block(a):
        a = a.reshape(N_META, c_heads, dv).transpose(1, 0, 2)
        return jnp.pad(a, ((0, 0), (0, LANES - N_META), (0, 0)))

    k_meta = meta_block(qkv_meta[:, d:2 * d])
    v_meta = meta_block(qkv_meta[:, 2 * d:])
    qkv = _norm_proj(h_real, norm_mix[1], w_qkv, tm=_pick_tile(seq, 512), tn=512, rope=rope_real, rope_cols=d)
    o = _diff_attn(qkv, k_meta, v_meta, c_lambda[0].astype(F32), c_subln[0].astype(F32), nb=bsz, seq=seq,
                   heads=c_heads, tq=tq, hp=2, lambda_init=lambda_init)
    out = _mix_mlp(h_real, [o], c_w_out[0].astype(BF16), norm_mlp[1], w1[1], w2[1], norm_final,
                   tm=tm_mlp, tf=tf)
    return out.reshape(bsz, seq, d)
```

```python
import functools
import math

import jax
import jax.numpy as jnp
import numpy as np
from jax import lax
from jax.experimental import pallas as pl
from jax.experimental.pallas import tpu as pltpu

F32 = jnp.float32
BF16 = jnp.bfloat16

N_META = 16
EPS = 1e-6
A_HEADS = 4
B_BLOCKS = 8
B_CONV = 4
LRU_C = 8.0
ROPE_THETA = 500000.0

LANES = 128
SUBLANES = 8
GATE_ROWS = 16
NEG = -1e30
VMEM_LIMIT_BYTES = 56 * 1024 * 1024

_NT = (((1,), (1,)), ((), ()))
_TN = (((0,), (0,)), ((), ()))


def _cparams(*sem):
    return pltpu.CompilerParams(dimension_semantics=sem, vmem_limit_bytes=VMEM_LIMIT_BYTES)


def _sigmoid(x):
    return 0.5 * jnp.tanh(0.5 * x) + 0.5


def _sqrt_nonneg(x):
    return jnp.where(x > 0.0, x * lax.rsqrt(x), 0.0)


def _log_sigmoid(x):
    return jnp.minimum(x, 0.0) - jnp.log1p(jnp.exp(-jnp.abs(x)))


def _softplus(x):
    return jnp.maximum(x, 0.0) + jnp.log1p(jnp.exp(-jnp.abs(x)))


def _rms_scale(x):
    return x * lax.rsqrt(jnp.mean(x * x, axis=-1, keepdims=True) + EPS)


def _pick_tile(n, pref):
    t = min(n, pref)
    while n % t:
        t //= 2
    return t


def _resident(shape):
    return pl.BlockSpec(shape, lambda *_: (0,) * len(shape), pipeline_mode=pl.Buffered(1))


def _norm_proj_body(*refs, n_w, has_rope, has_gates, tn, rope_cols):
    it = iter(refs)
    x_ref, g_ref = next(it), next(it)
    w_refs = [next(it) for _ in range(n_w)]
    if has_rope:
        cos_ref, sa_ref, sb_ref = next(it), next(it), next(it)
    if has_gates:
        wgc_ref, wgr_ref, bc_ref, br_ref = next(it), next(it), next(it), next(it)
    o_ref = next(it)
    if has_gates:
        gcol_ref, grow_ref = next(it), next(it)
    xn_ref = next(it)

    xn_ref[...] = (_rms_scale(x_ref[...]) * g_ref[...]).astype(BF16)
    chunks = [(w_ref, c0) for w_ref in w_refs for c0 in range(0, w_ref.shape[1], tn)]
    for c, (w_ref, c0) in enumerate(chunks):
        acc = jnp.dot(xn_ref[...], w_ref[:, c0:c0 + tn], preferred_element_type=F32)
        sec = (c * tn) // rope_cols if has_rope else None
        if has_rope and sec < cos_ref.shape[0]:
            cos, sa, sb = cos_ref[sec], sa_ref[sec], sb_ref[sec]
            for cc in range(tn // LANES):
                y = acc[:, cc * LANES:(cc + 1) * LANES]
                y = y * cos + pltpu.roll(y, LANES - 8, 1) * sa + pltpu.roll(y, 8, 1) * sb
                o_ref[:, c * tn + cc * LANES:c * tn + (cc + 1) * LANES] = y.astype(o_ref.dtype)
        else:
            o_ref[:, c * tn:(c + 1) * tn] = acc.astype(o_ref.dtype)

    if has_gates:
        xn = xn_ref[...]
        gc = jnp.dot(xn, wgc_ref[...], preferred_element_type=F32) + bc_ref[...]
        lane = lax.broadcasted_iota(jnp.int32, gc.shape, 1)
        gcol_ref[...] = jnp.where(lane < A_HEADS, gc, _log_sigmoid(gc))
        gr = lax.dot_general(wgr_ref[...], xn, _NT, preferred_element_type=F32) + br_ref[...]
        sub = lax.broadcasted_iota(jnp.int32, gr.shape, 0)
        grow_ref[...] = jnp.where(sub < A_HEADS, gr, _log_sigmoid(gr))


def _norm_proj(x, g, ws, *, tm, tn, rope=None, rope_cols=None, gates=None):
    n, d = x.shape
    nout = sum(w.shape[1] for w in ws)
    period = rope[0].shape[1] if rope is not None else n
    p_tiles = period // tm
    grid = (p_tiles, n // period)

    def rows(width):
        return pl.BlockSpec((tm, width), lambda p, r: (r * p_tiles + p, 0))

    in_specs = [rows(d), _resident((1, d))] + [_resident(w.shape) for w in ws]
    args = [x, g.reshape(1, d), *ws]
    if rope is not None:
        spec = pl.BlockSpec((rope[0].shape[0], tm, LANES), lambda p, r: (0, p, 0))
        in_specs += [spec, spec, spec]
        args += list(rope)
    out_shape = [jax.ShapeDtypeStruct((n, nout), BF16)]
    out_specs = [rows(nout)]
    if gates is not None:
        in_specs += [_resident(a.shape) for a in gates]
        args += list(gates)
        out_shape += [jax.ShapeDtypeStruct((n, LANES), F32), jax.ShapeDtypeStruct((GATE_ROWS, n), F32)]
        out_specs += [rows(LANES), pl.BlockSpec((GATE_ROWS, tm), lambda p, r: (0, r * p_tiles + p))]
    body = functools.partial(_norm_proj_body, n_w=len(ws), has_rope=rope is not None, has_gates=gates is not None,
                             tn=tn, rope_cols=rope_cols)
    outs = pl.pallas_call(
        body, grid=grid, in_specs=in_specs, out_specs=out_specs, out_shape=out_shape,
        scratch_shapes=[pltpu.VMEM((tm, d), BF16)],
        compiler_params=_cparams("parallel", "parallel"), name="norm_proj")(*args)
    return outs if gates is not None else outs[0]


def _mlstm_body(q_ref, k_ref, v_ref, o_ref, gcol_ref, grow_ref, nw_ref, c0_ref, m0_ref,
                y_ref, c_out, m_out, c_s, m_s, *, chunk, dh):
    ci = pl.program_id(1)

    @pl.when(ci == 0)
    def _():
        c_s[...] = c0_ref[...]
        m_s[...] = m0_ref[...]

    row = lax.broadcasted_iota(jnp.int32, (chunk, chunk), 0)
    col = lax.broadcasted_iota(jnp.int32, (chunk, chunk), 1)
    causal = col <= row
    bcol = jnp.dot(causal.astype(F32), gcol_ref[...], precision=lax.Precision.HIGHEST,
                   preferred_element_type=F32)
    brow = jnp.dot(grow_ref[...], (row <= col).astype(F32), precision=lax.Precision.HIGHEST,
                   preferred_element_type=F32)
    eye = (lax.broadcasted_iota(jnp.int32, (dh, dh), 0) == lax.broadcasted_iota(jnp.int32, (dh, dh), 1)).astype(BF16)
    ones = jnp.ones((chunk, LANES), BF16)
    scale = dh ** -0.5
    heads = range(A_HEADS)
    sls = [slice(h * dh, (h + 1) * dh) for h in heads]
    q = [q_ref[:, sl] for sl in sls]
    k = [k_ref[:, sl] * scale for sl in sls]
    v_aug = [jnp.concatenate([v_ref[:, sl], ones], axis=1) for sl in sls]
    b_c = [bcol[:, A_HEADS + h:A_HEADS + h + 1] for h in heads]
    li_r = [grow_ref[h:h + 1, :] for h in heads]
    b_r = [brow[A_HEADS + h:A_HEADS + h + 1, :] for h in heads]
    b_end = [b[chunk - 1:chunk, :] for b in b_c]
    m_prev = [m_s[h][:, 0:1] for h in heads]
    c_t = [c_s[h] for h in heads]

    qk = [lax.dot_general(q[h], k[h], _NT, preferred_element_type=F32) for h in heads]
    k_t = [lax.dot_general(eye, k[h], _NT, preferred_element_type=F32) for h in heads]
    inter = [jnp.dot(q[h], c_t[h].astype(BF16), preferred_element_type=F32) for h in heads]
    logd = [jnp.where(causal, b_c[h] - b_r[h] + li_r[h], NEG) for h in heads]
    log_prev = [b_c[h] + m_prev[h] for h in heads]
    m_t = [jnp.maximum(log_prev[h], jnp.max(logd[h], axis=1, keepdims=True)) for h in heads]
    s = [qk[h] * jnp.exp(logd[h] - m_t[h]) for h in heads]
    intra = [jnp.dot(s[h].astype(BF16), v_aug[h], preferred_element_type=F32) for h in heads]
    m_new = [jnp.maximum(b_end[h] + m_prev[h], jnp.max(b_end[h] - b_r[h] + li_r[h], axis=1, keepdims=True))
             for h in heads]
    kw_t = [(k_t[h] * jnp.exp(b_end[h] - b_r[h] + li_r[h] - m_new[h])).astype(BF16) for h in heads]
    upd = [jnp.dot(kw_t[h], v_aug[h], preferred_element_type=F32) for h in heads]
    for h in heads:
        num = jnp.exp(log_prev[h] - m_t[h]) * inter[h] + intra[h]
        den = num[:, dh:dh + 1]
        hh = num[:, :dh] / jnp.maximum(jnp.abs(den), jnp.exp(-m_t[h]))
        ha = _sigmoid(o_ref[:, sls[h]].astype(F32)) * hh
        y_ref[:, sls[h]] = (_rms_scale(ha) * nw_ref[:, sls[h]]).astype(y_ref.dtype)
        c_s[h] = jnp.exp(b_end[h] + m_prev[h] - m_new[h]) * c_t[h] + upd[h]
        m_s[h] = jnp.broadcast_to(m_new[h], (1, LANES))

    @pl.when(ci == pl.num_programs(1) - 1)
    def _():
        c_out[...] = c_s[...]
        m_out[...] = m_s[...]


def _mlstm(proj, gcol, grow, norm_w, state, *, nb, chunk, dh):
    n = gcol.shape[0]
    nc = n // nb // chunk
    width = A_HEADS * dh

    def tok(cb):
        return pl.BlockSpec((chunk, width), lambda b, c: (b * nc + c, cb))

    def whole(a):
        return pl.BlockSpec(a.shape, lambda b, c: (0,) * a.ndim)

    def per_seq(a):
        return pl.BlockSpec((None,) + a.shape, lambda b, c: (b,) + (0,) * a.ndim)

    nw = norm_w.reshape(1, width)
    in_specs = [tok(0), tok(1), tok(2), tok(3),
                pl.BlockSpec((chunk, LANES), lambda b, c: (b * nc + c, 0)),
                pl.BlockSpec((GATE_ROWS, chunk), lambda b, c: (0, b * nc + c)),
                whole(nw)] + [whole(a) for a in state]
    out_shape = [jax.ShapeDtypeStruct((n, width), BF16)] + [
        jax.ShapeDtypeStruct((nb,) + a.shape, F32) for a in state]
    out_specs = [pl.BlockSpec((chunk, width), lambda b, c: (b * nc + c, 0))] + [per_seq(a) for a in state]
    return pl.pallas_call(
        functools.partial(_mlstm_body, chunk=chunk, dh=dh),
        grid=(nb, nc), in_specs=in_specs, out_specs=out_specs, out_shape=out_shape,
        scratch_shapes=[pltpu.VMEM(a.shape, F32) for a in state],
        compiler_params=_cparams("parallel", "arbitrary"), name="mlstm",
    )(proj, proj, proj, proj, gcol, grow, nw, *state)


def _rglru_body(xb_ref, gate_ref, cw_ref, cb_ref, wri_ref, bri_ref, lam_ref, h0_ref, tail0_ref,
                y_ref, h_out, tail_out, h_s, xpad_s, *, tt, bd):
    @pl.when(pl.program_id(1) == 0)
    def _():
        h_s[...] = h0_ref[...]
        xpad_s[:SUBLANES, :] = tail0_ref[...]

    xpad_s[SUBLANES:, :] = xb_ref[...].astype(F32)
    row = lax.broadcasted_iota(jnp.int32, (tt, bd), 0)
    grp = min(tt, 4 * SUBLANES)
    for n in range(B_BLOCKS):
        sl = slice(n * bd, (n + 1) * bd)
        xc = cb_ref[:, sl]
        for s in range(B_CONV):
            xc = xc + xpad_s[SUBLANES - s:SUBLANES - s + tt, sl] * cw_ref[B_CONV - 1 - s:B_CONV - s, sl]

        ri = jnp.dot(xc.astype(BF16), wri_ref[n], preferred_element_type=F32) + bri_ref[n]
        r = _sigmoid(ri[:, :bd])
        i = _sigmoid(ri[:, bd:])
        log_a = (-LRU_C * _softplus(-lam_ref[:, sl])) * r
        a = jnp.exp(log_a)
        u = _sqrt_nonneg(1.0 - a * a) * (i * xc)
        sh = 1
        while sh < grp:
            if sh < SUBLANES:
                keep = row % grp >= sh
                u = jnp.where(keep, a * pltpu.roll(u, sh, 0) + u, u)
                a = jnp.where(keep, a * pltpu.roll(a, sh, 0), a)
            else:
                parts_u, parts_a = [], []
                for g0 in range(0, tt, grp):
                    ag, ug = a[g0:g0 + grp], u[g0:g0 + grp]
                    parts_u += [ug[:sh], ag[sh:] * ug[:grp - sh] + ug[sh:]]
                    parts_a += [ag[:sh], ag[sh:] * ag[:grp - sh]]
                u, a = jnp.concatenate(parts_u, axis=0), jnp.concatenate(parts_a, axis=0)
            sh *= 2
        carry = h_s[:, sl]
        pieces = []
        for g0 in range(0, tt, grp):
            hg = a[g0:g0 + grp] * carry + u[g0:g0 + grp]
            carry = hg[grp - 1:grp, :]
            pieces.append(hg)
        hh = jnp.concatenate(pieces, axis=0)
        h_s[:, sl] = carry
        y_ref[:, sl] = (hh * jax.nn.gelu(gate_ref[:, sl].astype(F32))).astype(y_ref.dtype)

    last8 = xpad_s[tt:, :]
    xpad_s[:SUBLANES, :] = last8

    @pl.when(pl.program_id(1) == pl.num_programs(1) - 1)
    def _():
        h_out[...] = h_s[...]
        tail_out[...] = last8


def _rglru(proj, xb_block, gate_block, conv_w, conv_b, w_ri, b_ri, lam, state, *, nb, tt):
    n = proj.shape[0]
    nt = n // nb // tt
    c = conv_w.shape[1]
    bd = c // B_BLOCKS
    h0, tail0 = state

    def whole(a):
        return pl.BlockSpec(a.shape, lambda b, t: (0,) * a.ndim)

    def per_seq(a):
        return pl.BlockSpec((None,) + a.shape, lambda b, t: (b,) + (0,) * a.ndim)

    consts = [conv_w, conv_b.reshape(1, c), w_ri, b_ri, lam.reshape(1, c), h0, tail0]
    in_specs = [pl.BlockSpec((tt, c), lambda b, t: (b * nt + t, xb_block)),
                pl.BlockSpec((tt, c), lambda b, t: (b * nt + t, gate_block))] + [whole(a) for a in consts]
    out_shape = [jax.ShapeDtypeStruct((n, c), BF16)] + [jax.ShapeDtypeStruct((nb,) + a.shape, F32) for a in state]
    out_specs = [pl.BlockSpec((tt, c), lambda b, t: (b * nt + t, 0))] + [per_seq(a) for a in state]
    return pl.pallas_call(
        functools.partial(_rglru_body, tt=tt, bd=bd),
        grid=(nb, nt), in_specs=in_specs, out_specs=out_specs, out_shape=out_shape,
        scratch_shapes=[pltpu.VMEM(h0.shape, F32), pltpu.VMEM((SUBLANES + tt, c), F32)],
        compiler_params=_cparams("parallel", "arbitrary"), name="rglru",
    )(proj, proj, *consts)


def _mix_mlp_body(*refs, n_y, final_norm, tf):
    it = iter(refs)
    x_ref = next(it)
    y_refs = [next(it) for _ in range(n_y)]
    wo_ref, g_ref, w1_ref, w2_ref = next(it), next(it), next(it), next(it)
    gf_ref = next(it) if final_norm else None
    o_ref, hn_s, acc_s = next(it), next(it), next(it)

    if n_y == 1:
        y = y_refs[0][...]
    else:
        ycat_s = next(it)
        off = 0
        for y_ref in y_refs:
            ycat_s[:, off:off + y_ref.shape[1]] = y_ref[...]
            off += y_ref.shape[1]
        y = ycat_s[...]
    acc_s[...] = x_ref[...] + jnp.dot(y, wo_ref[...], preferred_element_type=F32)
    hn_s[...] = (_rms_scale(acc_s[...]) * g_ref[...]).astype(BF16)
    for c in range(w1_ref.shape[1] // tf):
        a = jnp.maximum(jnp.dot(hn_s[...], w1_ref[:, c * tf:(c + 1) * tf], preferred_element_type=F32), 0.0)
        acc_s[...] += jnp.dot((a * a).astype(BF16), w2_ref[c * tf:(c + 1) * tf, :], preferred_element_type=F32)
    out = acc_s[...]
    if final_norm:
        out = _rms_scale(out) * gf_ref[...]
    o_ref[...] = out


def _mix_mlp(x, ys, wo, g, w1, w2, g_final=None, *, tm, tf):
    n, d = x.shape

    def rows(width):
        return pl.BlockSpec((tm, width), lambda i: (i, 0))

    in_specs = [rows(d)] + [rows(y.shape[1]) for y in ys]
    in_specs += [_resident(wo.shape), _resident((1, d)), _resident(w1.shape), _resident(w2.shape)]
    args = [x, *ys, wo, g.reshape(1, d), w1, w2]
    if g_final is not None:
        in_specs.append(_resident((1, d)))
        args.append(g_final.reshape(1, d))
    scratch = [pltpu.VMEM((tm, d), BF16), pltpu.VMEM((tm, d), F32)]
    if len(ys) > 1:
        scratch.append(pltpu.VMEM((tm, wo.shape[0]), BF16))
    return pl.pallas_call(
        functools.partial(_mix_mlp_body, n_y=len(ys), final_norm=g_final is not None, tf=tf),
        grid=(n // tm,), in_specs=in_specs, out_specs=rows(d),
        out_shape=jax.ShapeDtypeStruct((n, d), F32), scratch_shapes=scratch,
        compiler_params=_cparams("parallel"), name="mix_mlp")(*args)


def _diff_attn_body(q_ref, k_ref, v_ref, km_ref, vm_ref, lv_ref, sub_ref, o_ref, s_scr, mb_scr, acc_scr,
                    *, tq, dqk, hp, lambda_init):
    qi = pl.program_id(2)
    dv = 2 * dqk
    n_fold = tq // LANES
    lane_q = lax.broadcasted_iota(jnp.int32, (1, dv), 1)
    sel1 = (lane_q < dqk).astype(BF16)
    sel2 = (lane_q >= dqk).astype(BF16)
    meta_lane = lax.broadcasted_iota(jnp.int32, (2 * tq, LANES), 1) < N_META

    def head(ref, hh, rows=slice(None)):
        return ref[rows, hh * dv:(hh + 1) * dv]

    qs = []
    for hh in range(hp):
        q = head(q_ref, hh)
        qs.append(jnp.concatenate([q * sel1, q * sel2], axis=0))

    def scores(hh, k):
        return lax.dot_general(qs[hh], k, _NT, preferred_element_type=F32)

    def lane_fold(x, op):
        r = x[:, :LANES]
        for c in range(1, x.shape[1] // LANES):
            r = op(r, x[:, c * LANES:(c + 1) * LANES])
        return r

    def key_block(j, nblk=1):
        return pl.ds(pl.multiple_of(j * tq, tq), nblk * tq)

    def with_ones(v):
        return jnp.concatenate([v, jnp.ones((v.shape[0], LANES), v.dtype)], axis=1)

    mb_scr[...] = jnp.full(mb_scr.shape, NEG, F32)

    def pass1(j, nblk):
        for hh in range(hp):
            s = scores(hh, head(k_ref, hh, key_block(j, nblk)))
            for b in range(nblk):
                s_scr[hh, j + b] = s[:, b * tq:(b + 1) * tq]
            mb_scr[hh] = jnp.maximum(mb_scr[hh], lane_fold(s, jnp.maximum))

    def in_pairs(n_blocks, fn):
        def pair(t, carry):
            fn(2 * t, 2)
            return carry

        lax.fori_loop(0, n_blocks // 2, pair, 0)

        @pl.when(n_blocks % 2 == 1)
        def _():
            fn(n_blocks - 1, 1)

    in_pairs(qi, pass1)
    row = lax.broadcasted_iota(jnp.int32, (2 * tq, tq), 0)
    col = lax.broadcasted_iota(jnp.int32, (2 * tq, tq), 1)
    visible = col <= jnp.where(row >= tq, row - tq, row)
    s_diag = [jnp.where(visible, scores(hh, head(k_ref, hh, key_block(qi))), NEG) for hh in range(hp)]
    s_meta = [jnp.where(meta_lane, scores(hh, km_ref[hh]), NEG) for hh in range(hp)]
    for hh in range(hp):
        s_scr[hh, qi] = s_diag[hh]
        m_lanes = jnp.maximum(jnp.maximum(mb_scr[hh], s_meta[hh]), lane_fold(s_diag[hh], jnp.maximum))
        mb = jnp.broadcast_to(jnp.max(m_lanes, axis=1, keepdims=True), m_lanes.shape)
        mb_scr[hh] = mb
        p_meta = jnp.exp2(s_meta[hh] - mb)
        acc_scr[hh] = jnp.dot(p_meta.astype(BF16), with_ones(vm_ref[hh]), preferred_element_type=F32)

    def pass2(j, nblk):
        for hh in range(hp):
            mb = mb_scr[hh]
            p = jnp.concatenate(
                [jnp.exp2(s_scr[hh, j + b, :, c * LANES:(c + 1) * LANES] - mb)
                 for b in range(nblk) for c in range(n_fold)], axis=1)
            acc_scr[hh] += jnp.dot(p.astype(BF16), with_ones(head(v_ref, hh, key_block(j, nblk))),
                                   preferred_element_type=F32)

    in_pairs(qi + 1, pass2)
    lv = lv_ref[...]
    lam = (jnp.exp(jnp.sum(lv[0:1] * lv[1:2], axis=1, keepdims=True))
           - jnp.exp(jnp.sum(lv[2:3] * lv[3:4], axis=1, keepdims=True)) + lambda_init)
    for hh in range(hp):
        acc = acc_scr[hh]
        o = acc[:, :dv] / acc[:, dv:]
        od = o[:tq] - lam * o[tq:]
        od = _rms_scale(od) * sub_ref[...] * (1.0 - lambda_init)
        o_ref[:, hh * dv:(hh + 1) * dv] = od.astype(o_ref.dtype)


def _diff_attn(qkv, k_meta, v_meta, lam_vecs, subln, *, nb, seq, heads, tq, hp, lambda_init):
    n = qkv.shape[0]
    dv = subln.shape[0]
    nq = seq // tq
    hg = heads // hp
    w = hp * dv
    body = functools.partial(_diff_attn_body, tq=tq, dqk=dv // 2, hp=hp, lambda_init=lambda_init)
    stat = pltpu.VMEM((hp, 2 * tq, LANES), F32)
    acc = pltpu.VMEM((hp, 2 * tq, dv + LANES), F32)
    return pl.pallas_call(
        body, grid=(nb, hg, nq),
        in_specs=[pl.BlockSpec((tq, w), lambda b, h, i: (b * nq + i, h)),
                  pl.BlockSpec((seq, w), lambda b, h, i: (b, hg + h)),
                  pl.BlockSpec((seq, w), lambda b, h, i: (b, 2 * hg + h)),
                  pl.BlockSpec((hp, LANES, dv), lambda b, h, i: (h, 0, 0)),
                  pl.BlockSpec((hp, LANES, dv), lambda b, h, i: (h, 0, 0)),
                  pl.BlockSpec(lam_vecs.shape, lambda b, h, i: (0, 0)),
                  pl.BlockSpec((1, dv), lambda b, h, i: (0, 0))],
        out_specs=pl.BlockSpec((tq, w), lambda b, h, i: (b * nq + i, h)),
        out_shape=jax.ShapeDtypeStruct((n, heads * dv), BF16),
        scratch_shapes=[pltpu.VMEM((hp, nq, 2 * tq, tq), F32), stat, acc],
        compiler_params=_cparams("parallel", "parallel", "arbitrary"), name="diff_attn",
    )(qkv, qkv, qkv, k_meta, v_meta, lam_vecs, subln.reshape(1, dv))


def _rope_tables(positions, dqk, rot_dim):
    inv_freq = jnp.power(jnp.float32(ROPE_THETA), -jnp.arange(0, rot_dim, 2, dtype=F32) / rot_dim)
    half = rot_dim // 2
    c = np.arange(LANES) % dqk
    inv_lane = jnp.where(c < rot_dim, inv_freq[c % half], 0.0)
    ang = positions.astype(F32)[:, None] * inv_lane[None, :]
    cos, sin = jnp.cos(ang), jnp.sin(ang)
    sa = jnp.where(c < half, -sin, 0.0)
    sb = jnp.where(c >= half, sin, 0.0)
    qs = dqk ** -0.5 * math.log2(math.e)
    return jnp.stack([cos * qs, cos]), jnp.stack([sa * qs, sa]), jnp.stack([sb * qs, sb])


def kernel(x, meta_tokens, norm_mix, norm_mlp, norm_final, ab_w_in, ab_if_bias, mlstm_norm, lru_conv_w,
           lru_conv_b, lru_w_r, lru_b_r, lru_w_i, lru_b_i, lru_lambda, ab_w_out, c_w_in, c_lambda,
           c_subln, c_w_out, mlp_w1, mlp_w2):
    bsz, seq, d = x.shape
    n = bsz * seq
    assert norm_mix.shape[0] == 2 and meta_tokens.shape[0] == N_META
    a_width = mlstm_norm.shape[1]
    dh = a_width // A_HEADS
    b_width = lru_conv_w.shape[2]
    bd = b_width // B_BLOCKS
    assert a_width == b_width == d
    c_heads = d // LANES
    dv = c_subln.shape[1]
    dqk = dv // 2
    rot_dim = dqk // 4

    tm_proj = _pick_tile(n, 512)
    chunk = _pick_tile(seq, 256)
    tt = _pick_tile(seq, 512)
    tm_mlp = _pick_tile(n, 512)
    tf = 1024
    tq = _pick_tile(seq, 512)

    xf = x.reshape(n, d)
    meta = meta_tokens.astype(x.dtype)

    w_in = ab_w_in[0]
    n_qkvo = 4 * a_width
    w_main = [w_in[:, :n_qkvo].astype(BF16), w_in[:, n_qkvo + 2 * A_HEADS:].astype(BF16)]
    w_g = w_in[:, n_qkvo:n_qkvo + 2 * A_HEADS]
    wg_col = jnp.pad(w_g, ((0, 0), (0, LANES - 2 * A_HEADS))).astype(BF16)
    wg_row = jnp.pad(w_g.T, ((0, GATE_ROWS - 2 * A_HEADS), (0, 0))).astype(BF16)
    ifb = ab_if_bias[0].astype(F32)
    gates = (wg_col, wg_row, jnp.pad(ifb, (0, LANES - 2 * A_HEADS)).reshape(1, LANES),
             jnp.pad(ifb, (0, GATE_ROWS - 2 * A_HEADS)).reshape(GATE_ROWS, 1))
    w_ri = jnp.concatenate([lru_w_r[0], lru_w_i[0]], axis=-1).astype(BF16)
    b_ri = jnp.concatenate([lru_b_r[0].reshape(B_BLOCKS, 1, bd), lru_b_i[0].reshape(B_BLOCKS, 1, bd)],
                           axis=-1).astype(F32)
    wo_ab = ab_w_out[0].astype(BF16)
    w1 = [mlp_w1[l].astype(BF16) for l in range(2)]
    w2 = [mlp_w2[l].astype(BF16) for l in range(2)]

    def layer0(rows, nb, t_rows, tm, chunk_, tt_, state_a, state_b):
        proj, gcol, grow = _norm_proj(rows, norm_mix[0], w_main, tm=tm, tn=512, gates=gates)
        y_a, *fin_a = _mlstm(proj, gcol, grow, mlstm_norm[0], state_a, nb=nb, chunk=chunk_, dh=dh)
        y_b, *fin_b = _rglru(proj, 4, 5, lru_conv_w[0], lru_conv_b[0], w_ri, b_ri, lru_lambda[0], state_b,
                             nb=nb, tt=tt_)
        h = _mix_mlp(rows, [y_a, y_b], wo_ab, norm_mlp[0], w1[0], w2[0], tm=tm, tf=tf)
        return h, fin_a, fin_b

    zero_a = (jnp.zeros((A_HEADS, dh, dh + LANES), F32), jnp.zeros((A_HEADS, 1, LANES), F32))
    zero_b = (jnp.zeros((1, b_width), F32), jnp.zeros((SUBLANES, b_width), F32))
    h_meta, fin_a, fin_b = layer0(meta, 1, N_META, N_META, N_META, N_META, zero_a, zero_b)
    h_real, _, _ = layer0(xf, bsz, seq, tm_proj, chunk, tt, tuple(a[0] for a in fin_a),
                          tuple(a[0] for a in fin_b))

    lambda_init = 0.8 - 0.6 * math.exp(-0.3 * 1)
    w_qkv = c_w_in[0].astype(BF16)
    pos = jnp.arange(N_META + seq, dtype=jnp.int32)
    rope_meta = _rope_tables(pos[:N_META], dqk, rot_dim)
    rope_real = _rope_tables(pos[N_META:], dqk, rot_dim)
    qkv_meta = _norm_proj(h_meta, norm_mix[1], [w_qkv], tm=N_META, tn=512, rope=rope_meta, rope_cols=d)
    def meta_block(a):
        a = a.reshape(N_META, c_heads, dv).transpose(1, 0, 2)
        return jnp.pad(a, ((0, 0), (0, LANES - N_META), (0, 0)))

    k_meta = meta_block(qkv_meta[:, d:2 * d])
    v_meta = meta_block(qkv_meta[:, 2 * d:])
    qkv = _norm_proj(h_real, norm_mix[1], [w_qkv], tm=_pick_tile(seq, 512), tn=512, rope=rope_real, rope_cols=d)
    o = _diff_attn(qkv, k_meta, v_meta, c_lambda[0].astype(F32), c_subln[0].astype(F32), nb=bsz, seq=seq,
                   heads=c_heads, tq=tq, hp=2, lambda_init=lambda_init)
    out = _mix_mlp(h_real, [o], c_w_out[0].astype(BF16), norm_mlp[1], w1[1], w2[1], norm_final,
                   tm=tm_mlp, tf=tf)
    return out.reshape(bsz, seq, d)
```

```python
import functools
import math

import jax
import jax.numpy as jnp
import numpy as np
from jax import lax
from jax.experimental import pallas as pl
from jax.experimental.pallas import tpu as pltpu

F32 = jnp.float32
BF16 = jnp.bfloat16

N_META = 16
EPS = 1e-6
A_HEADS = 4
B_BLOCKS = 8
B_CONV = 4
LRU_C = 8.0
ROPE_THETA = 500000.0

LANES = 128
SUBLANES = 8
GATE_ROWS = 16
NEG = -1e30
VMEM_LIMIT_BYTES = 56 * 1024 * 1024

_NT = (((1,), (1,)), ((), ()))
_TN = (((0,), (0,)), ((), ()))


def _cparams(*sem):
    return pltpu.CompilerParams(dimension_semantics=sem, vmem_limit_bytes=VMEM_LIMIT_BYTES)


def _sigmoid(x):
    return 0.5 * jnp.tanh(0.5 * x) + 0.5


def _sqrt_nonneg(x):
    return jnp.where(x > 0.0, x * lax.rsqrt(x), 0.0)


def _log_sigmoid(x):
    return jnp.minimum(x, 0.0) - jnp.log1p(jnp.exp(-jnp.abs(x)))


def _softplus(x):
    return jnp.maximum(x, 0.0) + jnp.log1p(jnp.exp(-jnp.abs(x)))


def _rms_scale(x):
    return x * lax.rsqrt(jnp.mean(x * x, axis=-1, keepdims=True) + EPS)


def _pick_tile(n, pref):
    t = min(n, pref)
    while n % t:
        t //= 2
    return t


def _resident(shape):
    return pl.BlockSpec(shape, lambda *_: (0,) * len(shape), pipeline_mode=pl.Buffered(1))


def _norm_proj_body(*refs, n_w, has_rope, has_gates, tn, rope_cols):
    it = iter(refs)
    x_ref, g_ref = next(it), next(it)
    w_refs = [next(it) for _ in range(n_w)]
    if has_rope:
        cos_ref, sa_ref, sb_ref = next(it), next(it), next(it)
    if has_gates:
        wgc_ref, wgr_ref, bc_ref, br_ref = next(it), next(it), next(it), next(it)
    o_ref = next(it)
    if has_gates:
        gcol_ref, grow_ref = next(it), next(it)
    xn_ref = next(it)

    xn_ref[...] = (_rms_scale(x_ref[...]) * g_ref[...]).astype(BF16)
    chunks = [(w_ref, c0) for w_ref in w_refs for c0 in range(0, w_ref.shape[1], tn)]
    for c, (w_ref, c0) in enumerate(chunks):
        acc = jnp.dot(xn_ref[...], w_ref[:, c0:c0 + tn], preferred_element_type=F32)
        sec = (c * tn) // rope_cols if has_rope else None
        if has_rope and sec < cos_ref.shape[0]:
            cos, sa, sb = cos_ref[sec], sa_ref[sec], sb_ref[sec]
            for cc in range(tn // LANES):
                y = acc[:, cc * LANES:(cc + 1) * LANES]
                y = y * cos + pltpu.roll(y, LANES - 8, 1) * sa + pltpu.roll(y, 8, 1) * sb
                o_ref[:, c * tn + cc * LANES:c * tn + (cc + 1) * LANES] = y.astype(o_ref.dtype)
        else:
            o_ref[:, c * tn:(c + 1) * tn] = acc.astype(o_ref.dtype)

    if has_gates:
        xn = xn_ref[...]
        gc = jnp.dot(xn, wgc_ref[...], preferred_element_type=F32) + bc_ref[...]
        lane = lax.broadcasted_iota(jnp.int32, gc.shape, 1)
        gcol_ref[...] = jnp.where(lane < A_HEADS, gc, _log_sigmoid(gc))
        gr = lax.dot_general(wgr_ref[...], xn, _NT, preferred_element_type=F32) + br_ref[...]
        sub = lax.broadcasted_iota(jnp.int32, gr.shape, 0)
        grow_ref[...] = jnp.where(sub < A_HEADS, gr, _log_sigmoid(gr))


def _norm_proj(x, g, ws, *, tm, tn, rope=None, rope_cols=None, gates=None):
    n, d = x.shape
    nout = sum(w.shape[1] for w in ws)
    period = rope[0].shape[1] if rope is not None else n
    p_tiles = period // tm
    grid = (p_tiles, n // period)

    def rows(width):
        return pl.BlockSpec((tm, width), lambda p, r: (r * p_tiles + p, 0))

    in_specs = [rows(d), _resident((1, d))] + [_resident(w.shape) for w in ws]
    args = [x, g.reshape(1, d), *ws]
    if rope is not None:
        spec = pl.BlockSpec((rope[0].shape[0], tm, LANES), lambda p, r: (0, p, 0))
        in_specs += [spec, spec, spec]
        args += list(rope)
    out_shape = [jax.ShapeDtypeStruct((n, nout), BF16)]
    out_specs = [rows(nout)]
    if gates is not None:
        in_specs += [_resident(a.shape) for a in gates]
        args += list(gates)
        out_shape += [jax.ShapeDtypeStruct((n, LANES), F32), jax.ShapeDtypeStruct((GATE_ROWS, n), F32)]
        out_specs += [rows(LANES), pl.BlockSpec((GATE_ROWS, tm), lambda p, r: (0, r * p_tiles + p))]
    body = functools.partial(_norm_proj_body, n_w=len(ws), has_rope=rope is not None, has_gates=gates is not None,
                             tn=tn, rope_cols=rope_cols)
    outs = pl.pallas_call(
        body, grid=grid, in_specs=in_specs, out_specs=out_specs, out_shape=out_shape,
        scratch_shapes=[pltpu.VMEM((tm, d), BF16)],
        compiler_params=_cparams("parallel", "parallel"), name="norm_proj")(*args)
    return outs if gates is not None else outs[0]


def _mlstm_body(q_ref, k_ref, v_ref, o_ref, gcol_ref, grow_ref, nw_ref, c0_ref, m0_ref,
                y_ref, c_out, m_out, c_s, m_s, *, chunk, dh, gb):
    ci = pl.program_id(1)

    @pl.when(ci == 0)
    def _():
        for g in range(gb):
            c_s[g] = c0_ref[...]
            m_s[g] = m0_ref[...]

    row = lax.broadcasted_iota(jnp.int32, (chunk, chunk), 0)
    col = lax.broadcasted_iota(jnp.int32, (chunk, chunk), 1)
    causal = col <= row
    tril, triu = causal.astype(F32), (row <= col).astype(F32)
    bcol = [jnp.dot(tril, gcol_ref[g], precision=lax.Precision.HIGHEST, preferred_element_type=F32)
            for g in range(gb)]
    brow = [jnp.dot(grow_ref[g], triu, precision=lax.Precision.HIGHEST, preferred_element_type=F32)
            for g in range(gb)]
    eye = (lax.broadcasted_iota(jnp.int32, (dh, dh), 0) == lax.broadcasted_iota(jnp.int32, (dh, dh), 1)).astype(BF16)
    ones = jnp.ones((chunk, LANES), BF16)
    scale = dh ** -0.5
    units = [(g, h) for g in range(gb) for h in range(A_HEADS)]
    ids = range(len(units))
    sls = [slice(h * dh, (h + 1) * dh) for _, h in units]
    q = [q_ref[g, :, sls[u]] for u, (g, _) in enumerate(units)]
    k = [k_ref[g, :, sls[u]] * scale for u, (g, _) in enumerate(units)]
    v_aug = [jnp.concatenate([v_ref[g, :, sls[u]], ones], axis=1) for u, (g, _) in enumerate(units)]
    b_c = [bcol[g][:, A_HEADS + h:A_HEADS + h + 1] for g, h in units]
    li_r = [grow_ref[g, h:h + 1, :] for g, h in units]
    b_r = [brow[g][A_HEADS + h:A_HEADS + h + 1, :] for g, h in units]
    b_end = [b[chunk - 1:chunk, :] for b in b_c]
    m_prev = [m_s[g, h][:, 0:1] for g, h in units]
    c_t = [c_s[g, h] for g, h in units]

    qk = [lax.dot_general(q[u], k[u], _NT, preferred_element_type=F32) for u in ids]
    k_t = [lax.dot_general(eye, k[u], _NT, preferred_element_type=F32) for u in ids]
    inter = [jnp.dot(q[u], c_t[u].astype(BF16), preferred_element_type=F32) for u in ids]
    logd = [jnp.where(causal, b_c[u] - b_r[u] + li_r[u], NEG) for u in ids]
    log_prev = [b_c[u] + m_prev[u] for u in ids]
    m_t = [jnp.maximum(log_prev[u], jnp.max(logd[u], axis=1, keepdims=True)) for u in ids]
    s = [qk[u] * jnp.exp(logd[u] - m_t[u]) for u in ids]
    intra = [jnp.dot(s[u].astype(BF16), v_aug[u], preferred_element_type=F32) for u in ids]
    m_new = [jnp.maximum(b_end[u] + m_prev[u], jnp.max(b_end[u] - b_r[u] + li_r[u], axis=1, keepdims=True))
             for u in ids]
    kw_t = [(k_t[u] * jnp.exp(b_end[u] - b_r[u] + li_r[u] - m_new[u])).astype(BF16) for u in ids]
    upd = [jnp.dot(kw_t[u], v_aug[u], preferred_element_type=F32) for u in ids]
    for u, (g, h) in enumerate(units):
        num = jnp.exp(log_prev[u] - m_t[u]) * inter[u] + intra[u]
        den = num[:, dh:dh + 1]
        hh = num[:, :dh] / jnp.maximum(jnp.abs(den), jnp.exp(-m_t[u]))
        ha = _sigmoid(o_ref[g, :, sls[u]].astype(F32)) * hh
        y_ref[g, :, sls[u]] = (_rms_scale(ha) * nw_ref[:, sls[u]]).astype(y_ref.dtype)
        c_s[g, h] = jnp.exp(b_end[u] + m_prev[u] - m_new[u]) * c_t[u] + upd[u]
        m_s[g, h] = jnp.broadcast_to(m_new[u], (1, LANES))

    @pl.when(ci == pl.num_programs(1) - 1)
    def _():
        c_out[...] = c_s[...]
        m_out[...] = m_s[...]


def _mlstm(proj, gcol, grow, norm_w, state, *, nb, chunk, dh, gb):
    n = gcol.shape[0]
    t = n // nb
    nc = t // chunk
    width = A_HEADS * dh
    proj3 = proj.reshape(nb, t, proj.shape[1])
    gcol3 = gcol.reshape(nb, t, LANES)
    grow3 = grow.reshape(GATE_ROWS, nb, t).transpose(1, 0, 2)

    def tok(cb):
        return pl.BlockSpec((gb, chunk, width), lambda b, c: (b, c, cb))

    def whole(a):
        return pl.BlockSpec(a.shape, lambda b, c: (0,) * a.ndim)

    def per_seq(a):
        return pl.BlockSpec((gb,) + a.shape, lambda b, c: (b,) + (0,) * a.ndim)

    nw = norm_w.reshape(1, width)
    in_specs = [tok(0), tok(1), tok(2), tok(3),
                pl.BlockSpec((gb, chunk, LANES), lambda b, c: (b, c, 0)),
                pl.BlockSpec((gb, GATE_ROWS, chunk), lambda b, c: (b, 0, c)),
                whole(nw)] + [whole(a) for a in state]
    out_shape = [jax.ShapeDtypeStruct((nb, t, width), BF16)] + [
        jax.ShapeDtypeStruct((nb,) + a.shape, F32) for a in state]
    out_specs = [pl.BlockSpec((gb, chunk, width), lambda b, c: (b, c, 0))] + [per_seq(a) for a in state]
    y, *fin = pl.pallas_call(
        functools.partial(_mlstm_body, chunk=chunk, dh=dh, gb=gb),
        grid=(nb // gb, nc), in_specs=in_specs, out_specs=out_specs, out_shape=out_shape,
        scratch_shapes=[pltpu.VMEM((gb,) + a.shape, F32) for a in state],
        compiler_params=_cparams("parallel", "arbitrary"), name="mlstm",
    )(proj3, proj3, proj3, proj3, gcol3, grow3, nw, *state)
    return [y.reshape(n, width)] + fin


def _rglru_body(xb_ref, gate_ref, cw_ref, cb_ref, wri_ref, bri_ref, lam_ref, h0_ref, tail0_ref,
                y_ref, h_out, tail_out, h_s, xpad_s, *, tt, bd):
    @pl.when(pl.program_id(1) == 0)
    def _():
        h_s[...] = h0_ref[...]
        xpad_s[:SUBLANES, :] = tail0_ref[...]

    xpad_s[SUBLANES:, :] = xb_ref[...].astype(F32)
    row = lax.broadcasted_iota(jnp.int32, (tt, bd), 0)
    grp = min(tt, 4 * SUBLANES)
    for n in range(B_BLOCKS):
        sl = slice(n * bd, (n + 1) * bd)
        xc = cb_ref[:, sl]
        for s in range(B_CONV):
            xc = xc + xpad_s[SUBLANES - s:SUBLANES - s + tt, sl] * cw_ref[B_CONV - 1 - s:B_CONV - s, sl]

        ri = jnp.dot(xc.astype(BF16), wri_ref[n], preferred_element_type=F32) + bri_ref[n]
        r = _sigmoid(ri[:, :bd])
        i = _sigmoid(ri[:, bd:])
        log_a = (-LRU_C * _softplus(-lam_ref[:, sl])) * r
        a = jnp.exp(log_a)
        u = _sqrt_nonneg(1.0 - a * a) * (i * xc)
        sh = 1
        while sh < grp:
            if sh < SUBLANES:
                keep = row % grp >= sh
                u = jnp.where(keep, a * pltpu.roll(u, sh, 0) + u, u)
                a = jnp.where(keep, a * pltpu.roll(a, sh, 0), a)
            else:
                parts_u, parts_a = [], []
                for g0 in range(0, tt, grp):
                    ag, ug = a[g0:g0 + grp], u[g0:g0 + grp]
                    parts_u += [ug[:sh], ag[sh:] * ug[:grp - sh] + ug[sh:]]
                    parts_a += [ag[:sh], ag[sh:] * ag[:grp - sh]]
                u, a = jnp.concatenate(parts_u, axis=0), jnp.concatenate(parts_a, axis=0)
            sh *= 2
        carry = h_s[:, sl]
        pieces = []
        for g0 in range(0, tt, grp):
            hg = a[g0:g0 + grp] * carry + u[g0:g0 + grp]
            carry = hg[grp - 1:grp, :]
            pieces.append(hg)
        hh = jnp.concatenate(pieces, axis=0)
        h_s[:, sl] = carry
        y_ref[:, sl] = (hh * jax.nn.gelu(gate_ref[:, sl].astype(F32))).astype(y_ref.dtype)

    last8 = xpad_s[tt:, :]
    xpad_s[:SUBLANES, :] = last8

    @pl.when(pl.program_id(1) == pl.num_programs(1) - 1)
    def _():
        h_out[...] = h_s[...]
        tail_out[...] = last8


def _rglru(proj, xb_block, gate_block, conv_w, conv_b, w_ri, b_ri, lam, state, *, nb, tt):
    n = proj.shape[0]
    nt = n // nb // tt
    c = conv_w.shape[1]
    bd = c // B_BLOCKS
    h0, tail0 = state

    def whole(a):
        return pl.BlockSpec(a.shape, lambda b, t: (0,) * a.ndim)

    def per_seq(a):
        return pl.BlockSpec((None,) + a.shape, lambda b, t: (b,) + (0,) * a.ndim)

    consts = [conv_w, conv_b.reshape(1, c), w_ri, b_ri, lam.reshape(1, c), h0, tail0]
    in_specs = [pl.BlockSpec((tt, c), lambda b, t: (b * nt + t, xb_block)),
                pl.BlockSpec((tt, c), lambda b, t: (b * nt + t, gate_block))] + [whole(a) for a in consts]
    out_shape = [jax.ShapeDtypeStruct((n, c), BF16)] + [jax.ShapeDtypeStruct((nb,) + a.shape, F32) for a in state]
    out_specs = [pl.BlockSpec((tt, c), lambda b, t: (b * nt + t, 0))] + [per_seq(a) for a in state]
    return pl.pallas_call(
        functools.partial(_rglru_body, tt=tt, bd=bd),
        grid=(nb, nt), in_specs=in_specs, out_specs=out_specs, out_shape=out_shape,
        scratch_shapes=[pltpu.VMEM(h0.shape, F32), pltpu.VMEM((SUBLANES + tt, c), F32)],
        compiler_params=_cparams("parallel", "arbitrary"), name="rglru",
    )(proj, proj, *consts)


def _mix_mlp_body(*refs, n_y, final_norm, tf):
    it = iter(refs)
    x_ref = next(it)
    y_refs = [next(it) for _ in range(n_y)]
    wo_ref, g_ref, w1_ref, w2_ref = next(it), next(it), next(it), next(it)
    gf_ref = next(it) if final_norm else None
    o_ref, hn_s, acc_s = next(it), next(it), next(it)

    if n_y == 1:
        y = y_refs[0][...]
    else:
        ycat_s = next(it)
        off = 0
        for y_ref in y_refs:
            ycat_s[:, off:off + y_ref.shape[1]] = y_ref[...]
            off += y_ref.shape[1]
        y = ycat_s[...]
    acc_s[...] = x_ref[...] + jnp.dot(y, wo_ref[...], preferred_element_type=F32)
    hn_s[...] = (_rms_scale(acc_s[...]) * g_ref[...]).astype(BF16)
    for c in range(w1_ref.shape[1] // tf):
        a = jnp.maximum(jnp.dot(hn_s[...], w1_ref[:, c * tf:(c + 1) * tf], preferred_element_type=F32), 0.0)
        acc_s[...] += jnp.dot((a * a).astype(BF16), w2_ref[c * tf:(c + 1) * tf, :], preferred_element_type=F32)
    out = acc_s[...]
    if final_norm:
        out = _rms_scale(out) * gf_ref[...]
    o_ref[...] = out


def _mix_mlp(x, ys, wo, g, w1, w2, g_final=None, *, tm, tf):
    n, d = x.shape

    def rows(width):
        return pl.BlockSpec((tm, width), lambda i: (i, 0))

    in_specs = [rows(d)] + [rows(y.shape[1]) for y in ys]
    in_specs += [_resident(wo.shape), _resident((1, d)), _resident(w1.shape), _resident(w2.shape)]
    args = [x, *ys, wo, g.reshape(1, d), w1, w2]
    if g_final is not None:
        in_specs.append(_resident((1, d)))
        args.append(g_final.reshape(1, d))
    scratch = [pltpu.VMEM((tm, d), BF16), pltpu.VMEM((tm, d), F32)]
    if len(ys) > 1:
        scratch.append(pltpu.VMEM((tm, wo.shape[0]), BF16))
    return pl.pallas_call(
        functools.partial(_mix_mlp_body, n_y=len(ys), final_norm=g_final is not None, tf=tf),
        grid=(n // tm,), in_specs=in_specs, out_specs=rows(d),
        out_shape=jax.ShapeDtypeStruct((n, d), F32), scratch_shapes=scratch,
        compiler_params=_cparams("parallel"), name="mix_mlp")(*args)


def _diff_attn_body(q_ref, k_ref, v_ref, km_ref, vm_ref, lv_ref, sub_ref, o_ref, s_scr, mb_scr, acc_scr,
                    *, tq, dqk, hp, lambda_init):
    qi = pl.program_id(2)
    dv = 2 * dqk
    n_fold = tq // LANES
    lane_q = lax.broadcasted_iota(jnp.int32, (1, dv), 1)
    sel1 = (lane_q < dqk).astype(BF16)
    sel2 = (lane_q >= dqk).astype(BF16)
    meta_lane = lax.broadcasted_iota(jnp.int32, (2 * tq, LANES), 1) < N_META

    def head(ref, hh, rows=slice(None)):
        return ref[rows, hh * dv:(hh + 1) * dv]

    qs = []
    for hh in range(hp):
        q = head(q_ref, hh)
        qs.append(jnp.concatenate([q * sel1, q * sel2], axis=0))

    def scores(hh, k):
        return lax.dot_general(qs[hh], k, _NT, preferred_element_type=F32)

    def lane_fold(x, op):
        r = x[:, :LANES]
        for c in range(1, x.shape[1] // LANES):
            r = op(r, x[:, c * LANES:(c + 1) * LANES])
        return r

    def key_block(j, nblk=1):
        return pl.ds(pl.multiple_of(j * tq, tq), nblk * tq)

    def with_ones(v):
        return jnp.concatenate([v, jnp.ones((v.shape[0], LANES), v.dtype)], axis=1)

    mb_scr[...] = jnp.full(mb_scr.shape, NEG, F32)

    def pass1(j, nblk):
        for hh in range(hp):
            s = scores(hh, head(k_ref, hh, key_block(j, nblk)))
            for b in range(nblk):
                s_scr[hh, j + b] = s[:, b * tq:(b + 1) * tq]
            mb_scr[hh] = jnp.maximum(mb_scr[hh], lane_fold(s, jnp.maximum))

    def in_pairs(n_blocks, fn):
        def pair(t, carry):
            fn(2 * t, 2)
            return carry

        lax.fori_loop(0, n_blocks // 2, pair, 0)

        @pl.when(n_blocks % 2 == 1)
        def _():
            fn(n_blocks - 1, 1)

    in_pairs(qi, pass1)
    row = lax.broadcasted_iota(jnp.int32, (2 * tq, tq), 0)
    col = lax.broadcasted_iota(jnp.int32, (2 * tq, tq), 1)
    visible = col <= jnp.where(row >= tq, row - tq, row)
    s_diag = [jnp.where(visible, scores(hh, head(k_ref, hh, key_block(qi))), NEG) for hh in range(hp)]
    s_meta = [jnp.where(meta_lane, scores(hh, km_ref[hh]), NEG) for hh in range(hp)]
    for hh in range(hp):
        s_scr[hh, qi] = s_diag[hh]
        m_lanes = jnp.maximum(jnp.maximum(mb_scr[hh], s_meta[hh]), lane_fold(s_diag[hh], jnp.maximum))
        mb = jnp.broadcast_to(jnp.max(m_lanes, axis=1, keepdims=True), m_lanes.shape)
        mb_scr[hh] = mb
        p_meta = jnp.exp2(s_meta[hh] - mb)
        acc_scr[hh] = jnp.dot(p_meta.astype(BF16), with_ones(vm_ref[hh]), preferred_element_type=F32)

    def pass2(j, nblk):
        for hh in range(hp):
            mb = mb_scr[hh]
            p = jnp.concatenate(
                [jnp.exp2(s_scr[hh, j + b, :, c * LANES:(c + 1) * LANES] - mb)
                 for b in range(nblk) for c in range(n_fold)], axis=1)
            acc_scr[hh] += jnp.dot(p.astype(BF16), with_ones(head(v_ref, hh, key_block(j, nblk))),
                                   preferred_element_type=F32)

    in_pairs(qi + 1, pass2)
    lv = lv_ref[...]
    lam = (jnp.exp(jnp.sum(lv[0:1] * lv[1:2], axis=1, keepdims=True))
           - jnp.exp(jnp.sum(lv[2:3] * lv[3:4], axis=1, keepdims=True)) + lambda_init)
    for hh in range(hp):
        acc = acc_scr[hh]
        o = acc[:, :dv] / acc[:, dv:]
        od = o[:tq] - lam * o[tq:]
        od = _rms_scale(od) * sub_ref[...] * (1.0 - lambda_init)
        o_ref[:, hh * dv:(hh + 1) * dv] = od.astype(o_ref.dtype)


def _diff_attn(qkv, k_meta, v_meta, lam_vecs, subln, *, nb, seq, heads, tq, hp, lambda_init):
    n = qkv.shape[0]
    dv = subln.shape[0]
    nq = seq // tq
    hg = heads // hp
    w = hp * dv
    body = functools.partial(_diff_attn_body, tq=tq, dqk=dv // 2, hp=hp, lambda_init=lambda_init)
    stat = pltpu.VMEM((hp, 2 * tq, LANES), F32)
    acc = pltpu.VMEM((hp, 2 * tq, dv + LANES), F32)
    return pl.pallas_call(
        body, grid=(nb, hg, nq),
        in_specs=[pl.BlockSpec((tq, w), lambda b, h, i: (b * nq + i, h)),
                  pl.BlockSpec((seq, w), lambda b, h, i: (b, hg + h)),
                  pl.BlockSpec((seq, w), lambda b, h, i: (b, 2 * hg + h)),
                  pl.BlockSpec((hp, LANES, dv), lambda b, h, i: (h, 0, 0)),
                  pl.BlockSpec((hp, LANES, dv), lambda b, h, i: (h, 0, 0)),
                  pl.BlockSpec(lam_vecs.shape, lambda b, h, i: (0, 0)),
                  pl.BlockSpec((1, dv), lambda b, h, i: (0, 0))],
        out_specs=pl.BlockSpec((tq, w), lambda b, h, i: (b * nq + i, h)),
        out_shape=jax.ShapeDtypeStruct((n, heads * dv), BF16),
        scratch_shapes=[pltpu.VMEM((hp, nq, 2 * tq, tq), F32), stat, acc],
        compiler_params=_cparams("parallel", "parallel", "arbitrary"), name="diff_attn",
    )(qkv, qkv, qkv, k_meta, v_meta, lam_vecs, subln.reshape(1, dv))


def _rope_tables(positions, dqk, rot_dim):
    inv_freq = jnp.power(jnp.float32(ROPE_THETA), -jnp.arange(0, rot_dim, 2, dtype=F32) / rot_dim)
    half = rot_dim // 2
    c = np.arange(LANES) % dqk
    inv_lane = jnp.where(c < rot_dim, inv_freq[c % half], 0.0)
    ang = positions.astype(F32)[:, None] * inv_lane[None, :]
    cos, sin = jnp.cos(ang), jnp.sin(ang)
    sa = jnp.where(c < half, -sin, 0.0)
    sb = jnp.where(c >= half, sin, 0.0)
    qs = dqk ** -0.5 * math.log2(math.e)
    return jnp.stack([cos * qs, cos]), jnp.stack([sa * qs, sa]), jnp.stack([sb * qs, sb])


def kernel(x, meta_tokens, norm_mix, norm_mlp, norm_final, ab_w_in, ab_if_bias, mlstm_norm, lru_conv_w,
           lru_conv_b, lru_w_r, lru_b_r, lru_w_i, lru_b_i, lru_lambda, ab_w_out, c_w_in, c_lambda,
           c_subln, c_w_out, mlp_w1, mlp_w2):
    bsz, seq, d = x.shape
    n = bsz * seq
    assert norm_mix.shape[0] == 2 and meta_tokens.shape[0] == N_META
    a_width = mlstm_norm.shape[1]
    dh = a_width // A_HEADS
    b_width = lru_conv_w.shape[2]
    bd = b_width // B_BLOCKS
    assert a_width == b_width == d
    c_heads = d // LANES
    dv = c_subln.shape[1]
    dqk = dv // 2
    rot_dim = dqk // 4

    tm_proj = _pick_tile(n, 512)
    chunk = _pick_tile(seq, 256)
    tt = _pick_tile(seq, 512)
    tm_mlp = _pick_tile(n, 512)
    tf = 1024
    tq = _pick_tile(seq, 512)

    xf = x.reshape(n, d)
    meta = meta_tokens.astype(x.dtype)

    w_in = ab_w_in[0]
    n_qkvo = 4 * a_width
    w_main = [w_in[:, :n_qkvo].astype(BF16), w_in[:, n_qkvo + 2 * A_HEADS:].astype(BF16)]
    w_g = w_in[:, n_qkvo:n_qkvo + 2 * A_HEADS]
    wg_col = jnp.pad(w_g, ((0, 0), (0, LANES - 2 * A_HEADS))).astype(BF16)
    wg_row = jnp.pad(w_g.T, ((0, GATE_ROWS - 2 * A_HEADS), (0, 0))).astype(BF16)
    ifb = ab_if_bias[0].astype(F32)
    gates = (wg_col, wg_row, jnp.pad(ifb, (0, LANES - 2 * A_HEADS)).reshape(1, LANES),
             jnp.pad(ifb, (0, GATE_ROWS - 2 * A_HEADS)).reshape(GATE_ROWS, 1))
    w_ri = jnp.concatenate([lru_w_r[0], lru_w_i[0]], axis=-1).astype(BF16)
    b_ri = jnp.concatenate([lru_b_r[0].reshape(B_BLOCKS, 1, bd), lru_b_i[0].reshape(B_BLOCKS, 1, bd)],
                           axis=-1).astype(F32)
    wo_ab = ab_w_out[0].astype(BF16)
    w1 = [mlp_w1[l].astype(BF16) for l in range(2)]
    w2 = [mlp_w2[l].astype(BF16) for l in range(2)]

    def layer0(rows, nb, t_rows, tm, chunk_, tt_, state_a, state_b):
        proj, gcol, grow = _norm_proj(rows, norm_mix[0], w_main, tm=tm, tn=512, gates=gates)
        y_a, *fin_a = _mlstm(proj, gcol, grow, mlstm_norm[0], state_a, nb=nb, chunk=chunk_, dh=dh,
                             gb=2 if nb % 2 == 0 else 1)
        y_b, *fin_b = _rglru(proj, 4, 5, lru_conv_w[0], lru_conv_b[0], w_ri, b_ri, lru_lambda[0], state_b,
                             nb=nb, tt=tt_)
        h = _mix_mlp(rows, [y_a, y_b], wo_ab, norm_mlp[0], w1[0], w2[0], tm=tm, tf=tf)
        return h, fin_a, fin_b

    zero_a = (jnp.zeros((A_HEADS, dh, dh + LANES), F32), jnp.zeros((A_HEADS, 1, LANES), F32))
    zero_b = (jnp.zeros((1, b_width), F32), jnp.zeros((SUBLANES, b_width), F32))
    h_meta, fin_a, fin_b = layer0(meta, 1, N_META, N_META, N_META, N_META, zero_a, zero_b)
    h_real, _, _ = layer0(xf, bsz, seq, tm_proj, chunk, tt, tuple(a[0] for a in fin_a),
                          tuple(a[0] for a in fin_b))

    lambda_init = 0.8 - 0.6 * math.exp(-0.3 * 1)
    w_qkv = c_w_in[0].astype(BF16)
    pos = jnp.arange(N_META + seq, dtype=jnp.int32)
    rope_meta = _rope_tables(pos[:N_META], dqk, rot_dim)
    rope_real = _rope_tables(pos[N_META:], dqk, rot_dim)
    qkv_meta = _norm_proj(h_meta, norm_mix[1], [w_qkv], tm=N_META, tn=512, rope=rope_meta, rope_cols=d)
    def meta_block(a):
        a = a.reshape(N_META, c_heads, dv).transpose(1, 0, 2)
        return jnp.pad(a, ((0, 0), (0, LANES - N_META), (0, 0)))

    k_meta = meta_block(qkv_meta[:, d:2 * d])
    v_meta = meta_block(qkv_meta[:, 2 * d:])
    qkv = _norm_proj(h_real, norm_mix[1], [w_qkv], tm=_pick_tile(seq, 512), tn=512, rope=rope_real, rope_cols=d)
    o = _diff_attn(qkv, k_meta, v_meta, c_lambda[0].astype(F32), c_subln[0].astype(F32), nb=bsz, seq=seq,
                   heads=c_heads, tq=tq, hp=2, lambda_init=lambda_init)
    out = _mix_mlp(h_real, [o], c_w_out[0].astype(BF16), norm_mlp[1], w1[1], w2[1], norm_final,
                   tm=tm_mlp, tf=tf)
    return out.reshape(bsz, seq, d)
```

```python
import functools
import math

import jax
import jax.numpy as jnp
import numpy as np
from jax import lax
from jax.experimental import pallas as pl
from jax.experimental.pallas import tpu as pltpu

F32 = jnp.float32
BF16 = jnp.bfloat16

N_META = 16
EPS = 1e-6
A_HEADS = 4
B_BLOCKS = 8
B_CONV = 4
LRU_C = 8.0
ROPE_THETA = 500000.0

LANES = 128
SUBLANES = 8
GATE_ROWS = 16
NEG = -1e30
VMEM_LIMIT_BYTES = 56 * 1024 * 1024

_NT = (((1,), (1,)), ((), ()))
_TN = (((0,), (0,)), ((), ()))


def _cparams(*sem):
    return pltpu.CompilerParams(dimension_semantics=sem, vmem_limit_bytes=VMEM_LIMIT_BYTES)


def _sigmoid(x):
    return 0.5 * jnp.tanh(0.5 * x) + 0.5


def _sqrt_nonneg(x):
    return jnp.where(x > 0.0, x * lax.rsqrt(x), 0.0)


def _log_sigmoid(x):
    return jnp.minimum(x, 0.0) - jnp.log1p(jnp.exp(-jnp.abs(x)))


def _softplus(x):
    return jnp.maximum(x, 0.0) + jnp.log1p(jnp.exp(-jnp.abs(x)))


def _rms_scale(x):
    return x * lax.rsqrt(jnp.mean(x * x, axis=-1, keepdims=True) + EPS)


def _pick_tile(n, pref):
    t = min(n, pref)
    while n % t:
        t //= 2
    return t


def _resident(shape):
    return pl.BlockSpec(shape, lambda *_: (0,) * len(shape), pipeline_mode=pl.Buffered(1))


def _norm_proj_body(*refs, n_w, has_rope, has_gates, tn, rope_cols):
    it = iter(refs)
    x_ref, g_ref = next(it), next(it)
    w_refs = [next(it) for _ in range(n_w)]
    if has_rope:
        cos_ref, sa_ref, sb_ref = next(it), next(it), next(it)
    if has_gates:
        wgc_ref, wgr_ref, bc_ref, br_ref = next(it), next(it), next(it), next(it)
    o_ref = next(it)
    if has_gates:
        gcol_ref, grow_ref = next(it), next(it)
    xn_ref = next(it)

    xn_ref[...] = (_rms_scale(x_ref[...]) * g_ref[...]).astype(BF16)
    chunks = [(w_ref, c0) for w_ref in w_refs for c0 in range(0, w_ref.shape[1], tn)]
    for c, (w_ref, c0) in enumerate(chunks):
        acc = jnp.dot(xn_ref[...], w_ref[:, c0:c0 + tn], preferred_element_type=F32)
        sec = (c * tn) // rope_cols if has_rope else None
        if has_rope and sec < cos_ref.shape[0]:
            cos, sa, sb = cos_ref[sec], sa_ref[sec], sb_ref[sec]
            for cc in range(tn // LANES):
                y = acc[:, cc * LANES:(cc + 1) * LANES]
                y = y * cos + pltpu.roll(y, LANES - 8, 1) * sa + pltpu.roll(y, 8, 1) * sb
                o_ref[:, c * tn + cc * LANES:c * tn + (cc + 1) * LANES] = y.astype(o_ref.dtype)
        else:
            o_ref[:, c * tn:(c + 1) * tn] = acc.astype(o_ref.dtype)

    if has_gates:
        xn = xn_ref[...]
        gc = jnp.dot(xn, wgc_ref[...], preferred_element_type=F32) + bc_ref[...]
        lane = lax.broadcasted_iota(jnp.int32, gc.shape, 1)
        gcol_ref[...] = jnp.where(lane < A_HEADS, gc, _log_sigmoid(gc))
        gr = lax.dot_general(wgr_ref[...], xn, _NT, preferred_element_type=F32) + br_ref[...]
        sub = lax.broadcasted_iota(jnp.int32, gr.shape, 0)
        grow_ref[...] = jnp.where(sub < A_HEADS, gr, _log_sigmoid(gr))


def _norm_proj(x, g, ws, *, tm, tn, rope=None, rope_cols=None, gates=None):
    n, d = x.shape
    nout = sum(w.shape[1] for w in ws)
    period = rope[0].shape[1] if rope is not None else n
    p_tiles = period // tm
    grid = (p_tiles, n // period)

    def rows(width):
        return pl.BlockSpec((tm, width), lambda p, r: (r * p_tiles + p, 0))

    in_specs = [rows(d), _resident((1, d))] + [_resident(w.shape) for w in ws]
    args = [x, g.reshape(1, d), *ws]
    if rope is not None:
        spec = pl.BlockSpec((rope[0].shape[0], tm, LANES), lambda p, r: (0, p, 0))
        in_specs += [spec, spec, spec]
        args += list(rope)
    out_shape = [jax.ShapeDtypeStruct((n, nout), BF16)]
    out_specs = [rows(nout)]
    if gates is not None:
        in_specs += [_resident(a.shape) for a in gates]
        args += list(gates)
        out_shape += [jax.ShapeDtypeStruct((n, LANES), F32), jax.ShapeDtypeStruct((GATE_ROWS, n), F32)]
        out_specs += [rows(LANES), pl.BlockSpec((GATE_ROWS, tm), lambda p, r: (0, r * p_tiles + p))]
    body = functools.partial(_norm_proj_body, n_w=len(ws), has_rope=rope is not None, has_gates=gates is not None,
                             tn=tn, rope_cols=rope_cols)
    outs = pl.pallas_call(
        body, grid=grid, in_specs=in_specs, out_specs=out_specs, out_shape=out_shape,
        scratch_shapes=[pltpu.VMEM((tm, d), BF16)],
        compiler_params=_cparams("parallel", "parallel"), name="norm_proj")(*args)
    return outs if gates is not None else outs[0]


def _mlstm_body(q_ref, k_ref, v_ref, o_ref, gcol_ref, grow_ref, nw_ref, c0_ref, m0_ref,
                y_ref, c_out, m_out, c_s, m_s, *, chunk, dh):
    ci = pl.program_id(1)

    @pl.when(ci == 0)
    def _():
        c_s[...] = c0_ref[...]
        m_s[...] = m0_ref[...]

    row = lax.broadcasted_iota(jnp.int32, (chunk, chunk), 0)
    col = lax.broadcasted_iota(jnp.int32, (chunk, chunk), 1)
    causal = col <= row
    bcol = jnp.dot(causal.astype(F32), gcol_ref[...], precision=lax.Precision.HIGHEST,
                   preferred_element_type=F32)
    brow = jnp.dot(grow_ref[...], (row <= col).astype(F32), precision=lax.Precision.HIGHEST,
                   preferred_element_type=F32)
    eye = (lax.broadcasted_iota(jnp.int32, (dh, dh), 0) == lax.broadcasted_iota(jnp.int32, (dh, dh), 1)).astype(BF16)
    ones = jnp.ones((chunk, LANES), BF16)
    scale = dh ** -0.5
    heads = range(A_HEADS)
    sls = [slice(h * dh, (h + 1) * dh) for h in heads]
    q = [q_ref[:, sl] for sl in sls]
    k = [k_ref[:, sl] * scale for sl in sls]
    v_aug = [jnp.concatenate([v_ref[:, sl], ones], axis=1) for sl in sls]
    b_c = [bcol[:, A_HEADS + h:A_HEADS + h + 1] for h in heads]
    li_r = [grow_ref[h:h + 1, :] for h in heads]
    b_r = [brow[A_HEADS + h:A_HEADS + h + 1, :] for h in heads]
    b_end = [b[chunk - 1:chunk, :] for b in b_c]
    m_prev = [m_s[h][:, 0:1] for h in heads]
    c_t = [c_s[h] for h in heads]

    qk = [lax.dot_general(q[h], k[h], _NT, preferred_element_type=F32) for h in heads]
    k_t = [lax.dot_general(eye, k[h], _NT, preferred_element_type=F32) for h in heads]
    inter = [jnp.dot(q[h], c_t[h].astype(BF16), preferred_element_type=F32) for h in heads]
    logd = [jnp.where(causal, b_c[h] - b_r[h] + li_r[h], NEG) for h in heads]
    log_prev = [b_c[h] + m_prev[h] for h in heads]
    m_t = [jnp.maximum(log_prev[h], jnp.max(logd[h], axis=1, keepdims=True)) for h in heads]
    s = [qk[h] * jnp.exp(logd[h] - m_t[h]) for h in heads]
    intra = [jnp.dot(s[h].astype(BF16), v_aug[h], preferred_element_type=F32) for h in heads]
    m_new = [jnp.maximum(b_end[h] + m_prev[h], jnp.max(b_end[h] - b_r[h] + li_r[h], axis=1, keepdims=True))
             for h in heads]
    kw_t = [(k_t[h] * jnp.exp(b_end[h] - b_r[h] + li_r[h] - m_new[h])).astype(BF16) for h in heads]
    upd = [jnp.dot(kw_t[h], v_aug[h], preferred_element_type=F32) for h in heads]
    for h in heads:
        num = jnp.exp(log_prev[h] - m_t[h]) * inter[h] + intra[h]
        den = num[:, dh:dh + 1]
        hh = num[:, :dh] / jnp.maximum(jnp.abs(den), jnp.exp(-m_t[h]))
        ha = _sigmoid(o_ref[:, sls[h]].astype(F32)) * hh
        y_ref[:, sls[h]] = (_rms_scale(ha) * nw_ref[:, sls[h]]).astype(y_ref.dtype)
        c_s[h] = jnp.exp(b_end[h] + m_prev[h] - m_new[h]) * c_t[h] + upd[h]
        m_s[h] = jnp.broadcast_to(m_new[h], (1, LANES))

    @pl.when(ci == pl.num_programs(1) - 1)
    def _():
        c_out[...] = c_s[...]
        m_out[...] = m_s[...]


def _mlstm(proj, gcol, grow, norm_w, state, *, nb, chunk, dh):
    n = gcol.shape[0]
    nc = n // nb // chunk
    width = A_HEADS * dh

    def tok(cb):
        return pl.BlockSpec((chunk, width), lambda b, c: (b * nc + c, cb))

    def whole(a):
        return pl.BlockSpec(a.shape, lambda b, c: (0,) * a.ndim)

    def per_seq(a):
        return pl.BlockSpec((None,) + a.shape, lambda b, c: (b,) + (0,) * a.ndim)

    nw = norm_w.reshape(1, width)
    in_specs = [tok(0), tok(1), tok(2), tok(3),
                pl.BlockSpec((chunk, LANES), lambda b, c: (b * nc + c, 0)),
                pl.BlockSpec((GATE_ROWS, chunk), lambda b, c: (0, b * nc + c)),
                whole(nw)] + [whole(a) for a in state]
    out_shape = [jax.ShapeDtypeStruct((n, width), BF16)] + [
        jax.ShapeDtypeStruct((nb,) + a.shape, F32) for a in state]
    out_specs = [pl.BlockSpec((chunk, width), lambda b, c: (b * nc + c, 0))] + [per_seq(a) for a in state]
    return pl.pallas_call(
        functools.partial(_mlstm_body, chunk=chunk, dh=dh),
        grid=(nb, nc), in_specs=in_specs, out_specs=out_specs, out_shape=out_shape,
        scratch_shapes=[pltpu.VMEM(a.shape, F32) for a in state],
        compiler_params=_cparams("parallel", "arbitrary"), name="mlstm",
    )(proj, proj, proj, proj, gcol, grow, nw, *state)


def _rglru_body(xb_ref, gate_ref, cw_ref, cb_ref, wri_ref, bri_ref, lam_ref, h0_ref, tail0_ref,
                y_ref, h_out, tail_out, h_s, xpad_s, *, tt, bd):
    @pl.when(pl.program_id(1) == 0)
    def _():
        h_s[...] = h0_ref[...]
        xpad_s[:SUBLANES, :] = tail0_ref[...]

    xpad_s[SUBLANES:, :] = xb_ref[...].astype(F32)
    row = lax.broadcasted_iota(jnp.int32, (tt, bd), 0)
    grp = min(tt, 4 * SUBLANES)
    for n in range(B_BLOCKS):
        sl = slice(n * bd, (n + 1) * bd)
        xc = cb_ref[:, sl]
        for s in range(B_CONV):
            xc = xc + xpad_s[SUBLANES - s:SUBLANES - s + tt, sl] * cw_ref[B_CONV - 1 - s:B_CONV - s, sl]

        ri = jnp.dot(xc.astype(BF16), wri_ref[n], preferred_element_type=F32) + bri_ref[n]
        r = _sigmoid(ri[:, :bd])
        i = _sigmoid(ri[:, bd:])
        log_a = (-LRU_C * _softplus(-lam_ref[:, sl])) * r
        a = jnp.exp(log_a)
        u = _sqrt_nonneg(1.0 - a * a) * (i * xc)
        sh = 1
        while sh < grp:
            if sh < SUBLANES:
                keep = row % grp >= sh
                u = jnp.where(keep, a * pltpu.roll(u, sh, 0) + u, u)
                a = jnp.where(keep, a * pltpu.roll(a, sh, 0), a)
            else:
                parts_u, parts_a = [], []
                for g0 in range(0, tt, grp):
                    ag, ug = a[g0:g0 + grp], u[g0:g0 + grp]
                    parts_u += [ug[:sh], ag[sh:] * ug[:grp - sh] + ug[sh:]]
                    parts_a += [ag[:sh], ag[sh:] * ag[:grp - sh]]
                u, a = jnp.concatenate(parts_u, axis=0), jnp.concatenate(parts_a, axis=0)
            sh *= 2
        carry = h_s[:, sl]
        pieces = []
        for g0 in range(0, tt, grp):
            hg = a[g0:g0 + grp] * carry + u[g0:g0 + grp]
            carry = hg[grp - 1:grp, :]
            pieces.append(hg)
        hh = jnp.concatenate(pieces, axis=0)
        h_s[:, sl] = carry
        y_ref[:, sl] = (hh * jax.nn.gelu(gate_ref[:, sl].astype(F32))).astype(y_ref.dtype)

    last8 = xpad_s[tt:, :]
    xpad_s[:SUBLANES, :] = last8

    @pl.when(pl.program_id(1) == pl.num_programs(1) - 1)
    def _():
        h_out[...] = h_s[...]
        tail_out[...] = last8


def _rglru(proj, xb_block, gate_block, conv_w, conv_b, w_ri, b_ri, lam, state, *, nb, tt):
    n = proj.shape[0]
    nt = n // nb // tt
    c = conv_w.shape[1]
    bd = c // B_BLOCKS
    h0, tail0 = state

    def whole(a):
        return pl.BlockSpec(a.shape, lambda b, t: (0,) * a.ndim)

    def per_seq(a):
        return pl.BlockSpec((None,) + a.shape, lambda b, t: (b,) + (0,) * a.ndim)

    consts = [conv_w, conv_b.reshape(1, c), w_ri, b_ri, lam.reshape(1, c), h0, tail0]
    in_specs = [pl.BlockSpec((tt, c), lambda b, t: (b * nt + t, xb_block)),
                pl.BlockSpec((tt, c), lambda b, t: (b * nt + t, gate_block))] + [whole(a) for a in consts]
    out_shape = [jax.ShapeDtypeStruct((n, c), BF16)] + [jax.ShapeDtypeStruct((nb,) + a.shape, F32) for a in state]
    out_specs = [pl.BlockSpec((tt, c), lambda b, t: (b * nt + t, 0))] + [per_seq(a) for a in state]
    return pl.pallas_call(
        functools.partial(_rglru_body, tt=tt, bd=bd),
        grid=(nb, nt), in_specs=in_specs, out_specs=out_specs, out_shape=out_shape,
        scratch_shapes=[pltpu.VMEM(h0.shape, F32), pltpu.VMEM((SUBLANES + tt, c), F32)],
        compiler_params=_cparams("parallel", "arbitrary"), name="rglru",
    )(proj, proj, *consts)


def _mix_mlp_body(*refs, n_y, final_norm, tf):
    it = iter(refs)
    x_ref = next(it)
    y_refs = [next(it) for _ in range(n_y)]
    wo_ref, g_ref, w1_ref, w2_ref = next(it), next(it), next(it), next(it)
    gf_ref = next(it) if final_norm else None
    o_ref, hn_s, acc_s = next(it), next(it), next(it)

    if n_y == 1:
        y = y_refs[0][...]
    else:
        ycat_s = next(it)
        off = 0
        for y_ref in y_refs:
            ycat_s[:, off:off + y_ref.shape[1]] = y_ref[...]
            off += y_ref.shape[1]
        y = ycat_s[...]
    acc_s[...] = x_ref[...] + jnp.dot(y, wo_ref[...], preferred_element_type=F32)
    hn_s[...] = (_rms_scale(acc_s[...]) * g_ref[...]).astype(BF16)
    for c in range(w1_ref.shape[1] // tf):
        a = jnp.maximum(jnp.dot(hn_s[...], w1_ref[:, c * tf:(c + 1) * tf], preferred_element_type=F32), 0.0)
        acc_s[...] += jnp.dot((a * a).astype(BF16), w2_ref[c * tf:(c + 1) * tf, :], preferred_element_type=F32)
    out = acc_s[...]
    if final_norm:
        out = _rms_scale(out) * gf_ref[...]
    o_ref[...] = out


def _mix_mlp(x, ys, wo, g, w1, w2, g_final=None, *, tm, tf):
    n, d = x.shape

    def rows(width):
        return pl.BlockSpec((tm, width), lambda i: (i, 0))

    in_specs = [rows(d)] + [rows(y.shape[1]) for y in ys]
    in_specs += [_resident(wo.shape), _resident((1, d)), _resident(w1.shape), _resident(w2.shape)]
    args = [x, *ys, wo, g.reshape(1, d), w1, w2]
    if g_final is not None:
        in_specs.append(_resident((1, d)))
        args.append(g_final.reshape(1, d))
    scratch = [pltpu.VMEM((tm, d), BF16), pltpu.VMEM((tm, d), F32)]
    if len(ys) > 1:
        scratch.append(pltpu.VMEM((tm, wo.shape[0]), BF16))
    return pl.pallas_call(
        functools.partial(_mix_mlp_body, n_y=len(ys), final_norm=g_final is not None, tf=tf),
        grid=(n // tm,), in_specs=in_specs, out_specs=rows(d),
        out_shape=jax.ShapeDtypeStruct((n, d), F32), scratch_shapes=scratch,
        compiler_params=_cparams("parallel"), name="mix_mlp")(*args)


def _diff_attn_body(q_ref, k_ref, v_ref, km_ref, vm_ref, lv_ref, sub_ref, o_ref, s_scr, mb_scr, acc_scr,
                    *, tq, dqk, hp, lambda_init):
    qi = pl.program_id(2)
    dv = 2 * dqk
    n_fold = tq // LANES
    lane_q = lax.broadcasted_iota(jnp.int32, (1, dv), 1)
    sel1 = (lane_q < dqk).astype(BF16)
    sel2 = (lane_q >= dqk).astype(BF16)
    meta_lane = lax.broadcasted_iota(jnp.int32, (2 * tq, LANES), 1) < N_META

    def head(ref, hh, rows=slice(None)):
        return ref[rows, hh * dv:(hh + 1) * dv]

    qs = []
    for hh in range(hp):
        q = head(q_ref, hh)
        qs.append(jnp.concatenate([q * sel1, q * sel2], axis=0))

    def scores(hh, k):
        return lax.dot_general(qs[hh], k, _NT, preferred_element_type=F32)

    def lane_fold(x, op):
        r = x[:, :LANES]
        for c in range(1, x.shape[1] // LANES):
            r = op(r, x[:, c * LANES:(c + 1) * LANES])
        return r

    def key_block(j, nblk=1):
        return pl.ds(pl.multiple_of(j * tq, tq), nblk * tq)

    def with_ones(v):
        return jnp.concatenate([v, jnp.ones((v.shape[0], LANES), v.dtype)], axis=1)

    mb_scr[...] = jnp.full(mb_scr.shape, NEG, F32)

    def pass1(j, nblk):
        for hh in range(hp):
            s = scores(hh, head(k_ref, hh, key_block(j, nblk)))
            for b in range(nblk):
                s_scr[hh, j + b] = s[:, b * tq:(b + 1) * tq]
            mb_scr[hh] = jnp.maximum(mb_scr[hh], lane_fold(s, jnp.maximum))

    def in_groups(n_blocks, fn, group=3):
        def trip(t, carry):
            fn(group * t, group)
            return carry

        lax.fori_loop(0, n_blocks // group, trip, 0)
        rem = n_blocks % group
        for left in range(1, group):
            @pl.when(rem == left)
            def _(left=left):
                fn(n_blocks - left, left)

    in_groups(qi, pass1)
    row = lax.broadcasted_iota(jnp.int32, (2 * tq, tq), 0)
    col = lax.broadcasted_iota(jnp.int32, (2 * tq, tq), 1)
    visible = col <= jnp.where(row >= tq, row - tq, row)
    s_diag = [jnp.where(visible, scores(hh, head(k_ref, hh, key_block(qi))), NEG) for hh in range(hp)]
    s_meta = [jnp.where(meta_lane, scores(hh, km_ref[hh]), NEG) for hh in range(hp)]
    for hh in range(hp):
        s_scr[hh, qi] = s_diag[hh]
        m_lanes = jnp.maximum(jnp.maximum(mb_scr[hh], s_meta[hh]), lane_fold(s_diag[hh], jnp.maximum))
        mb = jnp.broadcast_to(jnp.max(m_lanes, axis=1, keepdims=True), m_lanes.shape)
        mb_scr[hh] = mb
        p_meta = jnp.exp2(s_meta[hh] - mb)
        acc_scr[hh] = jnp.dot(p_meta.astype(BF16), with_ones(vm_ref[hh]), preferred_element_type=F32)

    def pass2(j, nblk):
        for hh in range(hp):
            mb = mb_scr[hh]
            p = jnp.concatenate(
                [jnp.exp2(s_scr[hh, j + b, :, c * LANES:(c + 1) * LANES] - mb)
                 for b in range(nblk) for c in range(n_fold)], axis=1)
            acc_scr[hh] += jnp.dot(p.astype(BF16), with_ones(head(v_ref, hh, key_block(j, nblk))),
                                   preferred_element_type=F32)

    in_groups(qi + 1, pass2)
    lv = lv_ref[...]
    lam = (jnp.exp(jnp.sum(lv[0:1] * lv[1:2], axis=1, keepdims=True))
           - jnp.exp(jnp.sum(lv[2:3] * lv[3:4], axis=1, keepdims=True)) + lambda_init)
    for hh in range(hp):
        acc = acc_scr[hh]
        o = acc[:, :dv] / acc[:, dv:]
        od = o[:tq] - lam * o[tq:]
        od = _rms_scale(od) * sub_ref[...] * (1.0 - lambda_init)
        o_ref[:, hh * dv:(hh + 1) * dv] = od.astype(o_ref.dtype)


def _diff_attn(qkv, k_meta, v_meta, lam_vecs, subln, *, nb, seq, heads, tq, hp, lambda_init):
    n = qkv.shape[0]
    dv = subln.shape[0]
    nq = seq // tq
    hg = heads // hp
    w = hp * dv
    body = functools.partial(_diff_attn_body, tq=tq, dqk=dv // 2, hp=hp, lambda_init=lambda_init)
    stat = pltpu.VMEM((hp, 2 * tq, LANES), F32)
    acc = pltpu.VMEM((hp, 2 * tq, dv + LANES), F32)
    return pl.pallas_call(
        body, grid=(nb, hg, nq),
        in_specs=[pl.BlockSpec((tq, w), lambda b, h, i: (b * nq + i, h)),
                  pl.BlockSpec((seq, w), lambda b, h, i: (b, hg + h)),
                  pl.BlockSpec((seq, w), lambda b, h, i: (b, 2 * hg + h)),
                  pl.BlockSpec((hp, LANES, dv), lambda b, h, i: (h, 0, 0)),
                  pl.BlockSpec((hp, LANES, dv), lambda b, h, i: (h, 0, 0)),
                  pl.BlockSpec(lam_vecs.shape, lambda b, h, i: (0, 0)),
                  pl.BlockSpec((1, dv), lambda b, h, i: (0, 0))],
        out_specs=pl.BlockSpec((tq, w), lambda b, h, i: (b * nq + i, h)),
        out_shape=jax.ShapeDtypeStruct((n, heads * dv), BF16),
        scratch_shapes=[pltpu.VMEM((hp, nq, 2 * tq, tq), F32), stat, acc],
        compiler_params=_cparams("parallel", "parallel", "arbitrary"), name="diff_attn",
    )(qkv, qkv, qkv, k_meta, v_meta, lam_vecs, subln.reshape(1, dv))


def _rope_tables(positions, dqk, rot_dim):
    inv_freq = jnp.power(jnp.float32(ROPE_THETA), -jnp.arange(0, rot_dim, 2, dtype=F32) / rot_dim)
    half = rot_dim // 2
    c = np.arange(LANES) % dqk
    inv_lane = jnp.where(c < rot_dim, inv_freq[c % half], 0.0)
    ang = positions.astype(F32)[:, None] * inv_lane[None, :]
    cos, sin = jnp.cos(ang), jnp.sin(ang)
    sa = jnp.where(c < half, -sin, 0.0)
    sb = jnp.where(c >= half, sin, 0.0)
    qs = dqk ** -0.5 * math.log2(math.e)
    return jnp.stack([cos * qs, cos]), jnp.stack([sa * qs, sa]), jnp.stack([sb * qs, sb])


def kernel(x, meta_tokens, norm_mix, norm_mlp, norm_final, ab_w_in, ab_if_bias, mlstm_norm, lru_conv_w,
           lru_conv_b, lru_w_r, lru_b_r, lru_w_i, lru_b_i, lru_lambda, ab_w_out, c_w_in, c_lambda,
           c_subln, c_w_out, mlp_w1, mlp_w2):
    bsz, seq, d = x.shape
    n = bsz * seq
    assert norm_mix.shape[0] == 2 and meta_tokens.shape[0] == N_META
    a_width = mlstm_norm.shape[1]
    dh = a_width // A_HEADS
    b_width = lru_conv_w.shape[2]
    bd = b_width // B_BLOCKS
    assert a_width == b_width == d
    c_heads = d // LANES
    dv = c_subln.shape[1]
    dqk = dv // 2
    rot_dim = dqk // 4

    tm_proj = _pick_tile(n, 512)
    chunk = _pick_tile(seq, 256)
    tt = _pick_tile(seq, 512)
    tm_mlp = _pick_tile(n, 512)
    tf = 1024
    tq = _pick_tile(seq, 512)

    xf = x.reshape(n, d)
    meta = meta_tokens.astype(x.dtype)

    w_in = ab_w_in[0]
    n_qkvo = 4 * a_width
    w_main = [w_in[:, :n_qkvo].astype(BF16), w_in[:, n_qkvo + 2 * A_HEADS:].astype(BF16)]
    w_g = w_in[:, n_qkvo:n_qkvo + 2 * A_HEADS]
    wg_col = jnp.pad(w_g, ((0, 0), (0, LANES - 2 * A_HEADS))).astype(BF16)
    wg_row = jnp.pad(w_g.T, ((0, GATE_ROWS - 2 * A_HEADS), (0, 0))).astype(BF16)
    ifb = ab_if_bias[0].astype(F32)
    gates = (wg_col, wg_row, jnp.pad(ifb, (0, LANES - 2 * A_HEADS)).reshape(1, LANES),
             jnp.pad(ifb, (0, GATE_ROWS - 2 * A_HEADS)).reshape(GATE_ROWS, 1))
    w_ri = jnp.concatenate([lru_w_r[0], lru_w_i[0]], axis=-1).astype(BF16)
    b_ri = jnp.concatenate([lru_b_r[0].reshape(B_BLOCKS, 1, bd), lru_b_i[0].reshape(B_BLOCKS, 1, bd)],
                           axis=-1).astype(F32)
    wo_ab = ab_w_out[0].astype(BF16)
    w1 = [mlp_w1[l].astype(BF16) for l in range(2)]
    w2 = [mlp_w2[l].astype(BF16) for l in range(2)]

    def layer0(rows, nb, t_rows, tm, chunk_, tt_, state_a, state_b):
        proj, gcol, grow = _norm_proj(rows, norm_mix[0], w_main, tm=tm, tn=512, gates=gates)
        y_a, *fin_a = _mlstm(proj, gcol, grow, mlstm_norm[0], state_a, nb=nb, chunk=chunk_, dh=dh)
        y_b, *fin_b = _rglru(proj, 4, 5, lru_conv_w[0], lru_conv_b[0], w_ri, b_ri, lru_lambda[0], state_b,
                             nb=nb, tt=tt_)
        h = _mix_mlp(rows, [y_a, y_b], wo_ab, norm_mlp[0], w1[0], w2[0], tm=tm, tf=tf)
        return h, fin_a, fin_b

    zero_a = (jnp.zeros((A_HEADS, dh, dh + LANES), F32), jnp.zeros((A_HEADS, 1, LANES), F32))
    zero_b = (jnp.zeros((1, b_width), F32), jnp.zeros((SUBLANES, b_width), F32))
    h_meta, fin_a, fin_b = layer0(meta, 1, N_META, N_META, N_META, N_META, zero_a, zero_b)
    h_real, _, _ = layer0(xf, bsz, seq, tm_proj, chunk, tt, tuple(a[0] for a in fin_a),
                          tuple(a[0] for a in fin_b))

    lambda_init = 0.8 - 0.6 * math.exp(-0.3 * 1)
    w_qkv = c_w_in[0].astype(BF16)
    pos = jnp.arange(N_META + seq, dtype=jnp.int32)
    rope_meta = _rope_tables(pos[:N_META], dqk, rot_dim)
    rope_real = _rope_tables(pos[N_META:], dqk, rot_dim)
    qkv_meta = _norm_proj(h_meta, norm_mix[1], [w_qkv], tm=N_META, tn=512, rope=rope_meta, rope_cols=d)
    def meta_block(a):
        a = a.reshape(N_META, c_heads, dv).transpose(1, 0, 2)
        return jnp.pad(a, ((0, 0), (0, LANES - N_META), (0, 0)))

    k_meta = meta_block(qkv_meta[:, d:2 * d])
    v_meta = meta_block(qkv_meta[:, 2 * d:])
    qkv = _norm_proj(h_real, norm_mix[1], [w_qkv], tm=_pick_tile(seq, 512), tn=512, rope=rope_real, rope_cols=d)
    o = _diff_attn(qkv, k_meta, v_meta, c_lambda[0].astype(F32), c_subln[0].astype(F32), nb=bsz, seq=seq,
                   heads=c_heads, tq=tq, hp=2, lambda_init=lambda_init)
    out = _mix_mlp(h_real, [o], c_w_out[0].astype(BF16), norm_mlp[1], w1[1], w2[1], norm_final,
                   tm=tm_mlp, tf=tf)
    return out.reshape(bsz, seq, d)
```

```python
import functools
import math

import jax
import jax.numpy as jnp
import numpy as np
from jax import lax
from jax.experimental import pallas as pl
from jax.experimental.pallas import tpu as pltpu

F32 = jnp.float32
BF16 = jnp.bfloat16

N_META = 16
EPS = 1e-6
A_HEADS = 4
B_BLOCKS = 8
B_CONV = 4
LRU_C = 8.0
ROPE_THETA = 500000.0

LANES = 128
SUBLANES = 8
GATE_ROWS = 16
NEG = -1e30
VMEM_LIMIT_BYTES = 56 * 1024 * 1024

_NT = (((1,), (1,)), ((), ()))
_TN = (((0,), (0,)), ((), ()))


def _cparams(*sem):
    return pltpu.CompilerParams(dimension_semantics=sem, vmem_limit_bytes=VMEM_LIMIT_BYTES)


def _sigmoid(x):
    return 0.5 * jnp.tanh(0.5 * x) + 0.5


def _sqrt_nonneg(x):
    return jnp.where(x > 0.0, x * lax.rsqrt(x), 0.0)


def _log_sigmoid(x):
    return jnp.minimum(x, 0.0) - jnp.log1p(jnp.exp(-jnp.abs(x)))


def _softplus(x):
    return jnp.maximum(x, 0.0) + jnp.log1p(jnp.exp(-jnp.abs(x)))


def _rms_scale(x):
    return x * lax.rsqrt(jnp.mean(x * x, axis=-1, keepdims=True) + EPS)


def _pick_tile(n, pref):
    t = min(n, pref)
    while n % t:
        t //= 2
    return t


def _resident(shape):
    return pl.BlockSpec(shape, lambda *_: (0,) * len(shape), pipeline_mode=pl.Buffered(1))


def _norm_proj_body(*refs, n_w, has_rope, has_gates, tn, rope_cols):
    it = iter(refs)
    x_ref, g_ref = next(it), next(it)
    w_refs = [next(it) for _ in range(n_w)]
    if has_rope:
        cos_ref, sa_ref, sb_ref = next(it), next(it), next(it)
    if has_gates:
        wgc_ref, wgr_ref, bc_ref, br_ref = next(it), next(it), next(it), next(it)
    o_ref = next(it)
    if has_gates:
        gcol_ref, grow_ref = next(it), next(it)
    xn_ref = next(it)

    xn_ref[...] = (_rms_scale(x_ref[...]) * g_ref[...]).astype(BF16)
    chunks = [(w_ref, c0) for w_ref in w_refs for c0 in range(0, w_ref.shape[1], tn)]
    for c, (w_ref, c0) in enumerate(chunks):
        acc = jnp.dot(xn_ref[...], w_ref[:, c0:c0 + tn], preferred_element_type=F32)
        sec = (c * tn) // rope_cols if has_rope else None
        if has_rope and sec < cos_ref.shape[0]:
            cos, sa, sb = cos_ref[sec], sa_ref[sec], sb_ref[sec]
            for cc in range(tn // LANES):
                y = acc[:, cc * LANES:(cc + 1) * LANES]
                y = y * cos + pltpu.roll(y, LANES - 8, 1) * sa + pltpu.roll(y, 8, 1) * sb
                o_ref[:, c * tn + cc * LANES:c * tn + (cc + 1) * LANES] = y.astype(o_ref.dtype)
        else:
            o_ref[:, c * tn:(c + 1) * tn] = acc.astype(o_ref.dtype)

    if has_gates:
        xn = xn_ref[...]
        gc = jnp.dot(xn, wgc_ref[...], preferred_element_type=F32) + bc_ref[...]
        lane = lax.broadcasted_iota(jnp.int32, gc.shape, 1)
        gcol_ref[...] = jnp.where(lane < A_HEADS, gc, _log_sigmoid(gc))
        gr = lax.dot_general(wgr_ref[...], xn, _NT, preferred_element_type=F32) + br_ref[...]
        sub = lax.broadcasted_iota(jnp.int32, gr.shape, 0)
        grow_ref[...] = jnp.where(sub < A_HEADS, gr, _log_sigmoid(gr))


def _norm_proj(x, g, ws, *, tm, tn, rope=None, rope_cols=None, gates=None):
    n, d = x.shape
    nout = sum(w.shape[1] for w in ws)
    period = rope[0].shape[1] if rope is not None else n
    p_tiles = period // tm
    grid = (p_tiles, n // period)

    def rows(width):
        return pl.BlockSpec((tm, width), lambda p, r: (r * p_tiles + p, 0))

    in_specs = [rows(d), _resident((1, d))] + [_resident(w.shape) for w in ws]
    args = [x, g.reshape(1, d), *ws]
    if rope is not None:
        spec = pl.BlockSpec((rope[0].shape[0], tm, LANES), lambda p, r: (0, p, 0))
        in_specs += [spec, spec, spec]
        args += list(rope)
    out_shape = [jax.ShapeDtypeStruct((n, nout), BF16)]
    out_specs = [rows(nout)]
    if gates is not None:
        in_specs += [_resident(a.shape) for a in gates]
        args += list(gates)
        out_shape += [jax.ShapeDtypeStruct((n, LANES), F32), jax.ShapeDtypeStruct((GATE_ROWS, n), F32)]
        out_specs += [rows(LANES), pl.BlockSpec((GATE_ROWS, tm), lambda p, r: (0, r * p_tiles + p))]
    body = functools.partial(_norm_proj_body, n_w=len(ws), has_rope=rope is not None, has_gates=gates is not None,
                             tn=tn, rope_cols=rope_cols)
    outs = pl.pallas_call(
        body, grid=grid, in_specs=in_specs, out_specs=out_specs, out_shape=out_shape,
        scratch_shapes=[pltpu.VMEM((tm, d), BF16)],
        compiler_params=_cparams("parallel", "parallel"), name="norm_proj")(*args)
    return outs if gates is not None else outs[0]


def _mlstm_body(q_ref, k_ref, v_ref, o_ref, gcol_ref, grow_ref, nw_ref, c0_ref, m0_ref,
                y_ref, c_out, m_out, c_s, m_s, *, chunk, dh):
    ci = pl.program_id(1)

    @pl.when(ci == 0)
    def _():
        c_s[...] = c0_ref[...]
        m_s[...] = m0_ref[...]

    row = lax.broadcasted_iota(jnp.int32, (chunk, chunk), 0)
    col = lax.broadcasted_iota(jnp.int32, (chunk, chunk), 1)
    causal = col <= row
    bcol = jnp.dot(causal.astype(F32), gcol_ref[...], precision=lax.Precision.HIGHEST,
                   preferred_element_type=F32)
    brow = jnp.dot(grow_ref[...], (row <= col).astype(F32), precision=lax.Precision.HIGHEST,
                   preferred_element_type=F32)
    eye = (lax.broadcasted_iota(jnp.int32, (dh, dh), 0) == lax.broadcasted_iota(jnp.int32, (dh, dh), 1)).astype(BF16)
    ones = jnp.ones((chunk, LANES), BF16)
    scale = dh ** -0.5
    heads = range(A_HEADS)
    sls = [slice(h * dh, (h + 1) * dh) for h in heads]
    q = [q_ref[:, sl] for sl in sls]
    k = [k_ref[:, sl] * scale for sl in sls]
    v_aug = [jnp.concatenate([v_ref[:, sl], ones], axis=1) for sl in sls]
    b_c = [bcol[:, A_HEADS + h:A_HEADS + h + 1] for h in heads]
    li_r = [grow_ref[h:h + 1, :] for h in heads]
    b_r = [brow[A_HEADS + h:A_HEADS + h + 1, :] for h in heads]
    b_end = [b[chunk - 1:chunk, :] for b in b_c]
    m_prev = [m_s[h][:, 0:1] for h in heads]
    c_t = [c_s[h] for h in heads]

    qk = [lax.dot_general(q[h], k[h], _NT, preferred_element_type=F32) for h in heads]
    k_t = [lax.dot_general(eye, k[h], _NT, preferred_element_type=F32) for h in heads]
    inter = [jnp.dot(q[h], c_t[h].astype(BF16), preferred_element_type=F32) for h in heads]
    logd = [jnp.where(causal, b_c[h] - b_r[h] + li_r[h], NEG) for h in heads]
    log_prev = [b_c[h] + m_prev[h] for h in heads]
    m_t = [jnp.maximum(log_prev[h], jnp.max(logd[h], axis=1, keepdims=True)) for h in heads]
    s = [qk[h] * jnp.exp(logd[h] - m_t[h]) for h in heads]
    intra = [jnp.dot(s[h].astype(BF16), v_aug[h], preferred_element_type=F32) for h in heads]
    m_new = [jnp.maximum(b_end[h] + m_prev[h], jnp.max(b_end[h] - b_r[h] + li_r[h], axis=1, keepdims=True))
             for h in heads]
    kw_t = [(k_t[h] * jnp.exp(b_end[h] - b_r[h] + li_r[h] - m_new[h])).astype(BF16) for h in heads]
    upd = [jnp.dot(kw_t[h], v_aug[h], preferred_element_type=F32) for h in heads]
    for h in heads:
        num = jnp.exp(log_prev[h] - m_t[h]) * inter[h] + intra[h]
        den = num[:, dh:dh + 1]
        hh = num[:, :dh] / jnp.maximum(jnp.abs(den), jnp.exp(-m_t[h]))
        ha = _sigmoid(o_ref[:, sls[h]].astype(F32)) * hh
        y_ref[:, sls[h]] = (_rms_scale(ha) * nw_ref[:, sls[h]]).astype(y_ref.dtype)
        c_s[h] = jnp.exp(b_end[h] + m_prev[h] - m_new[h]) * c_t[h] + upd[h]
        m_s[h] = jnp.broadcast_to(m_new[h], (1, LANES))

    @pl.when(ci == pl.num_programs(1) - 1)
    def _():
        c_out[...] = c_s[...]
        m_out[...] = m_s[...]


def _mlstm(proj, gcol, grow, norm_w, state, *, nb, chunk, dh):
    n = gcol.shape[0]
    nc = n // nb // chunk
    width = A_HEADS * dh

    def tok(cb):
        return pl.BlockSpec((chunk, width), lambda b, c: (b * nc + c, cb))

    def whole(a):
        return pl.BlockSpec(a.shape, lambda b, c: (0,) * a.ndim)

    def per_seq(a):
        return pl.BlockSpec((None,) + a.shape, lambda b, c: (b,) + (0,) * a.ndim)

    nw = norm_w.reshape(1, width)
    in_specs = [tok(0), tok(1), tok(2), tok(3),
                pl.BlockSpec((chunk, LANES), lambda b, c: (b * nc + c, 0)),
                pl.BlockSpec((GATE_ROWS, chunk), lambda b, c: (0, b * nc + c)),
                whole(nw)] + [whole(a) for a in state]
    out_shape = [jax.ShapeDtypeStruct((n, width), BF16)] + [
        jax.ShapeDtypeStruct((nb,) + a.shape, F32) for a in state]
    out_specs = [pl.BlockSpec((chunk, width), lambda b, c: (b * nc + c, 0))] + [per_seq(a) for a in state]
    return pl.pallas_call(
        functools.partial(_mlstm_body, chunk=chunk, dh=dh),
        grid=(nb, nc), in_specs=in_specs, out_specs=out_specs, out_shape=out_shape,
        scratch_shapes=[pltpu.VMEM(a.shape, F32) for a in state],
        compiler_params=_cparams("parallel", "arbitrary"), name="mlstm",
    )(proj, proj, proj, proj, gcol, grow, nw, *state)


def _rglru_body(xb_ref, gate_ref, cw_ref, cb_ref, wri_ref, bri_ref, lam_ref, h0_ref, tail0_ref,
                y_ref, h_out, tail_out, h_s, xpad_s, *, tt, bd):
    @pl.when(pl.program_id(1) == 0)
    def _():
        h_s[...] = h0_ref[...]
        xpad_s[:SUBLANES, :] = tail0_ref[...]

    xpad_s[SUBLANES:, :] = xb_ref[...].astype(F32)
    row = lax.broadcasted_iota(jnp.int32, (tt, bd), 0)
    grp = min(tt, 4 * SUBLANES)
    for n in range(B_BLOCKS):
        sl = slice(n * bd, (n + 1) * bd)
        xc = cb_ref[:, sl]
        for s in range(B_CONV):
            xc = xc + xpad_s[SUBLANES - s:SUBLANES - s + tt, sl] * cw_ref[B_CONV - 1 - s:B_CONV - s, sl]

        ri = jnp.dot(xc.astype(BF16), wri_ref[n], preferred_element_type=F32) + bri_ref[n]
        r = _sigmoid(ri[:, :bd])
        i = _sigmoid(ri[:, bd:])
        log_a = (-LRU_C * _softplus(-lam_ref[:, sl])) * r
        a = jnp.exp(log_a)
        u = _sqrt_nonneg(1.0 - a * a) * (i * xc)
        sh = 1
        while sh < grp:
            if sh < SUBLANES:
                keep = row % grp >= sh
                u = jnp.where(keep, a * pltpu.roll(u, sh, 0) + u, u)
                a = jnp.where(keep, a * pltpu.roll(a, sh, 0), a)
            else:
                parts_u, parts_a = [], []
                for g0 in range(0, tt, grp):
                    ag, ug = a[g0:g0 + grp], u[g0:g0 + grp]
                    parts_u += [ug[:sh], ag[sh:] * ug[:grp - sh] + ug[sh:]]
                    parts_a += [ag[:sh], ag[sh:] * ag[:grp - sh]]
                u, a = jnp.concatenate(parts_u, axis=0), jnp.concatenate(parts_a, axis=0)
            sh *= 2
        carry = h_s[:, sl]
        pieces = []
        for g0 in range(0, tt, grp):
            hg = a[g0:g0 + grp] * carry + u[g0:g0 + grp]
            carry = hg[grp - 1:grp, :]
            pieces.append(hg)
        hh = jnp.concatenate(pieces, axis=0)
        h_s[:, sl] = carry
        y_ref[:, sl] = (hh * jax.nn.gelu(gate_ref[:, sl].astype(F32))).astype(y_ref.dtype)

    last8 = xpad_s[tt:, :]
    xpad_s[:SUBLANES, :] = last8

    @pl.when(pl.program_id(1) == pl.num_programs(1) - 1)
    def _():
        h_out[...] = h_s[...]
        tail_out[...] = last8


def _rglru(proj, xb_block, gate_block, conv_w, conv_b, w_ri, b_ri, lam, state, *, nb, tt):
    n = proj.shape[0]
    nt = n // nb // tt
    c = conv_w.shape[1]
    bd = c // B_BLOCKS
    h0, tail0 = state

    def whole(a):
        return pl.BlockSpec(a.shape, lambda b, t: (0,) * a.ndim)

    def per_seq(a):
        return pl.BlockSpec((None,) + a.shape, lambda b, t: (b,) + (0,) * a.ndim)

    consts = [conv_w, conv_b.reshape(1, c), w_ri, b_ri, lam.reshape(1, c), h0, tail0]
    in_specs = [pl.BlockSpec((tt, c), lambda b, t: (b * nt + t, xb_block)),
                pl.BlockSpec((tt, c), lambda b, t: (b * nt + t, gate_block))] + [whole(a) for a in consts]
    out_shape = [jax.ShapeDtypeStruct((n, c), BF16)] + [jax.ShapeDtypeStruct((nb,) + a.shape, F32) for a in state]
    out_specs = [pl.BlockSpec((tt, c), lambda b, t: (b * nt + t, 0))] + [per_seq(a) for a in state]
    return pl.pallas_call(
        functools.partial(_rglru_body, tt=tt, bd=bd),
        grid=(nb, nt), in_specs=in_specs, out_specs=out_specs, out_shape=out_shape,
        scratch_shapes=[pltpu.VMEM(h0.shape, F32), pltpu.VMEM((SUBLANES + tt, c), F32)],
        compiler_params=_cparams("parallel", "arbitrary"), name="rglru",
    )(proj, proj, *consts)


def _mix_mlp_body(*refs, n_y, final_norm, tf):
    it = iter(refs)
    x_ref = next(it)
    y_refs = [next(it) for _ in range(n_y)]
    wo_ref, g_ref, w1_ref, w2_ref = next(it), next(it), next(it), next(it)
    gf_ref = next(it) if final_norm else None
    o_ref, hn_s, acc_s = next(it), next(it), next(it)

    if n_y == 1:
        y = y_refs[0][...]
    else:
        ycat_s = next(it)
        off = 0
        for y_ref in y_refs:
            ycat_s[:, off:off + y_ref.shape[1]] = y_ref[...]
            off += y_ref.shape[1]
        y = ycat_s[...]
    acc_s[...] = x_ref[...] + jnp.dot(y, wo_ref[...], preferred_element_type=F32)
    hn_s[...] = (_rms_scale(acc_s[...]) * g_ref[...]).astype(BF16)
    for c in range(w1_ref.shape[1] // tf):
        a = jnp.maximum(jnp.dot(hn_s[...], w1_ref[:, c * tf:(c + 1) * tf], preferred_element_type=F32), 0.0)
        acc_s[...] += jnp.dot((a * a).astype(BF16), w2_ref[c * tf:(c + 1) * tf, :], preferred_element_type=F32)
    out = acc_s[...]
    if final_norm:
        out = _rms_scale(out) * gf_ref[...]
    o_ref[...] = out


def _mix_mlp(x, ys, wo, g, w1, w2, g_final=None, *, tm, tf):
    n, d = x.shape

    def rows(width):
        return pl.BlockSpec((tm, width), lambda i: (i, 0))

    in_specs = [rows(d)] + [rows(y.shape[1]) for y in ys]
    in_specs += [_resident(wo.shape), _resident((1, d)), _resident(w1.shape), _resident(w2.shape)]
    args = [x, *ys, wo, g.reshape(1, d), w1, w2]
    if g_final is not None:
        in_specs.append(_resident((1, d)))
        args.append(g_final.reshape(1, d))
    scratch = [pltpu.VMEM((tm, d), BF16), pltpu.VMEM((tm, d), F32)]
    if len(ys) > 1:
        scratch.append(pltpu.VMEM((tm, wo.shape[0]), BF16))
    return pl.pallas_call(
        functools.partial(_mix_mlp_body, n_y=len(ys), final_norm=g_final is not None, tf=tf),
        grid=(n // tm,), in_specs=in_specs, out_specs=rows(d),
        out_shape=jax.ShapeDtypeStruct((n, d), F32), scratch_shapes=scratch,
        compiler_params=_cparams("parallel"), name="mix_mlp")(*args)


def _diff_attn_body(q_ref, k_ref, v_ref, km_ref, vm_ref, lv_ref, sub_ref, o_ref, s_scr, mb_scr, acc_scr,
                    *, tq, dqk, hp, lambda_init):
    qi = pl.program_id(2)
    dv = 2 * dqk
    n_fold = tq // LANES
    lane_q = lax.broadcasted_iota(jnp.int32, (1, dv), 1)
    sel1 = (lane_q < dqk).astype(BF16)
    sel2 = (lane_q >= dqk).astype(BF16)
    meta_lane = lax.broadcasted_iota(jnp.int32, (2 * tq, LANES), 1) < N_META

    def head(ref, hh, rows=slice(None)):
        return ref[rows, hh * dv:(hh + 1) * dv]

    qs = []
    for hh in range(hp):
        q = head(q_ref, hh)
        qs.append(jnp.concatenate([q * sel1, q * sel2], axis=0))

    def scores(hh, k):
        return lax.dot_general(qs[hh], k, _NT, preferred_element_type=F32)

    def lane_fold(x, op):
        r = x[:, :LANES]
        for c in range(1, x.shape[1] // LANES):
            r = op(r, x[:, c * LANES:(c + 1) * LANES])
        return r

    def key_block(j, nblk=1):
        return pl.ds(pl.multiple_of(j * tq, tq), nblk * tq)

    def with_ones(v):
        return jnp.concatenate([v, jnp.ones((v.shape[0], LANES), v.dtype)], axis=1)

    mb_scr[...] = jnp.full(mb_scr.shape, NEG, F32)

    def pass1(j, nblk):
        for hh in range(hp):
            s = scores(hh, head(k_ref, hh, key_block(j, nblk)))
            for b in range(nblk):
                s_scr[hh, j + b] = s[:, b * tq:(b + 1) * tq]
            mb_scr[hh] = jnp.maximum(mb_scr[hh], lane_fold(s, jnp.maximum))

    def in_groups(n_blocks, fn, group=4):
        def trip(t, carry):
            fn(group * t, group)
            return carry

        lax.fori_loop(0, n_blocks // group, trip, 0)
        rem = n_blocks % group
        for left in range(1, group):
            @pl.when(rem == left)
            def _(left=left):
                fn(n_blocks - left, left)

    in_groups(qi, pass1)
    row = lax.broadcasted_iota(jnp.int32, (2 * tq, tq), 0)
    col = lax.broadcasted_iota(jnp.int32, (2 * tq, tq), 1)
    visible = col <= jnp.where(row >= tq, row - tq, row)
    s_diag = [jnp.where(visible, scores(hh, head(k_ref, hh, key_block(qi))), NEG) for hh in range(hp)]
    s_meta = [jnp.where(meta_lane, scores(hh, km_ref[hh]), NEG) for hh in range(hp)]
    for hh in range(hp):
        s_scr[hh, qi] = s_diag[hh]
        m_lanes = jnp.maximum(jnp.maximum(mb_scr[hh], s_meta[hh]), lane_fold(s_diag[hh], jnp.maximum))
        mb = jnp.broadcast_to(jnp.max(m_lanes, axis=1, keepdims=True), m_lanes.shape)
        mb_scr[hh] = mb
        p_meta = jnp.exp2(s_meta[hh] - mb)
        acc_scr[hh] = jnp.dot(p_meta.astype(BF16), with_ones(vm_ref[hh]), preferred_element_type=F32)

    def pass2(j, nblk):
        for hh in range(hp):
            mb = mb_scr[hh]
            p = jnp.concatenate(
                [jnp.exp2(s_scr[hh, j + b, :, c * LANES:(c + 1) * LANES] - mb)
                 for b in range(nblk) for c in range(n_fold)], axis=1)
            acc_scr[hh] += jnp.dot(p.astype(BF16), with_ones(head(v_ref, hh, key_block(j, nblk))),
                                   preferred_element_type=F32)

    in_groups(qi + 1, pass2)
    lv = lv_ref[...]
    lam = (jnp.exp(jnp.sum(lv[0:1] * lv[1:2], axis=1, keepdims=True))
           - jnp.exp(jnp.sum(lv[2:3] * lv[3:4], axis=1, keepdims=True)) + lambda_init)
    for hh in range(hp):
        acc = acc_scr[hh]
        o = acc[:, :dv] / acc[:, dv:]
        od = o[:tq] - lam * o[tq:]
        od = _rms_scale(od) * sub_ref[...] * (1.0 - lambda_init)
        o_ref[:, hh * dv:(hh + 1) * dv] = od.astype(o_ref.dtype)


def _diff_attn(qkv, k_meta, v_meta, lam_vecs, subln, *, nb, seq, heads, tq, hp, lambda_init):
    n = qkv.shape[0]
    dv = subln.shape[0]
    nq = seq // tq
    hg = heads // hp
    w = hp * dv
    body = functools.partial(_diff_attn_body, tq=tq, dqk=dv // 2, hp=hp, lambda_init=lambda_init)
    stat = pltpu.VMEM((hp, 2 * tq, LANES), F32)
    acc = pltpu.VMEM((hp, 2 * tq, dv + LANES), F32)
    return pl.pallas_call(
        body, grid=(nb, hg, nq),
        in_specs=[pl.BlockSpec((tq, w), lambda b, h, i: (b * nq + i, h)),
                  pl.BlockSpec((seq, w), lambda b, h, i: (b, hg + h)),
                  pl.BlockSpec((seq, w), lambda b, h, i: (b, 2 * hg + h)),
                  pl.BlockSpec((hp, LANES, dv), lambda b, h, i: (h, 0, 0)),
                  pl.BlockSpec((hp, LANES, dv), lambda b, h, i: (h, 0, 0)),
                  pl.BlockSpec(lam_vecs.shape, lambda b, h, i: (0, 0)),
                  pl.BlockSpec((1, dv), lambda b, h, i: (0, 0))],
        out_specs=pl.BlockSpec((tq, w), lambda b, h, i: (b * nq + i, h)),
        out_shape=jax.ShapeDtypeStruct((n, heads * dv), BF16),
        scratch_shapes=[pltpu.VMEM((hp, nq, 2 * tq, tq), F32), stat, acc],
        compiler_params=_cparams("parallel", "parallel", "arbitrary"), name="diff_attn",
    )(qkv, qkv, qkv, k_meta, v_meta, lam_vecs, subln.reshape(1, dv))


def _rope_tables(positions, dqk, rot_dim):
    inv_freq = jnp.power(jnp.float32(ROPE_THETA), -jnp.arange(0, rot_dim, 2, dtype=F32) / rot_dim)
    half = rot_dim // 2
    c = np.arange(LANES) % dqk
    inv_lane = jnp.where(c < rot_dim, inv_freq[c % half], 0.0)
    ang = positions.astype(F32)[:, None] * inv_lane[None, :]
    cos, sin = jnp.cos(ang), jnp.sin(ang)
    sa = jnp.where(c < half, -sin, 0.0)
    sb = jnp.where(c >= half, sin, 0.0)
    qs = dqk ** -0.5 * math.log2(math.e)
    return jnp.stack([cos * qs, cos]), jnp.stack([sa * qs, sa]), jnp.stack([sb * qs, sb])


def kernel(x, meta_tokens, norm_mix, norm_mlp, norm_final, ab_w_in, ab_if_bias, mlstm_norm, lru_conv_w,
           lru_conv_b, lru_w_r, lru_b_r, lru_w_i, lru_b_i, lru_lambda, ab_w_out, c_w_in, c_lambda,
           c_subln, c_w_out, mlp_w1, mlp_w2):
    bsz, seq, d = x.shape
    n = bsz * seq
    assert norm_mix.shape[0] == 2 and meta_tokens.shape[0] == N_META
    a_width = mlstm_norm.shape[1]
    dh = a_width // A_HEADS
    b_width = lru_conv_w.shape[2]
    bd = b_width // B_BLOCKS
    assert a_width == b_width == d
    c_heads = d // LANES
    dv = c_subln.shape[1]
    dqk = dv // 2
    rot_dim = dqk // 4

    tm_proj = _pick_tile(n, 512)
    chunk = _pick_tile(seq, 256)
    tt = _pick_tile(seq, 512)
    tm_mlp = _pick_tile(n, 512)
    tf = 1024
    tq = _pick_tile(seq, 512)

    xf = x.reshape(n, d)
    meta = meta_tokens.astype(x.dtype)

    w_in = ab_w_in[0]
    n_qkvo = 4 * a_width
    w_main = [w_in[:, :n_qkvo].astype(BF16), w_in[:, n_qkvo + 2 * A_HEADS:].astype(BF16)]
    w_g = w_in[:, n_qkvo:n_qkvo + 2 * A_HEADS]
    wg_col = jnp.pad(w_g, ((0, 0), (0, LANES - 2 * A_HEADS))).astype(BF16)
    wg_row = jnp.pad(w_g.T, ((0, GATE_ROWS - 2 * A_HEADS), (0, 0))).astype(BF16)
    ifb = ab_if_bias[0].astype(F32)
    gates = (wg_col, wg_row, jnp.pad(ifb, (0, LANES - 2 * A_HEADS)).reshape(1, LANES),
             jnp.pad(ifb, (0, GATE_ROWS - 2 * A_HEADS)).reshape(GATE_ROWS, 1))
    w_ri = jnp.concatenate([lru_w_r[0], lru_w_i[0]], axis=-1).astype(BF16)
    b_ri = jnp.concatenate([lru_b_r[0].reshape(B_BLOCKS, 1, bd), lru_b_i[0].reshape(B_BLOCKS, 1, bd)],
                           axis=-1).astype(F32)
    wo_ab = ab_w_out[0].astype(BF16)
    w1 = [mlp_w1[l].astype(BF16) for l in range(2)]
    w2 = [mlp_w2[l].astype(BF16) for l in range(2)]

    def layer0(rows, nb, t_rows, tm, chunk_, tt_, state_a, state_b):
        proj, gcol, grow = _norm_proj(rows, norm_mix[0], w_main, tm=tm, tn=512, gates=gates)
        y_a, *fin_a = _mlstm(proj, gcol, grow, mlstm_norm[0], state_a, nb=nb, chunk=chunk_, dh=dh)
        y_b, *fin_b = _rglru(proj, 4, 5, lru_conv_w[0], lru_conv_b[0], w_ri, b_ri, lru_lambda[0], state_b,
                             nb=nb, tt=tt_)
        h = _mix_mlp(rows, [y_a, y_b], wo_ab, norm_mlp[0], w1[0], w2[0], tm=tm, tf=tf)
        return h, fin_a, fin_b

    zero_a = (jnp.zeros((A_HEADS, dh, dh + LANES), F32), jnp.zeros((A_HEADS, 1, LANES), F32))
    zero_b = (jnp.zeros((1, b_width), F32), jnp.zeros((SUBLANES, b_width), F32))
    h_meta, fin_a, fin_b = layer0(meta, 1, N_META, N_META, N_META, N_META, zero_a, zero_b)
    h_real, _, _ = layer0(xf, bsz, seq, tm_proj, chunk, tt, tuple(a[0] for a in fin_a),
                          tuple(a[0] for a in fin_b))

    lambda_init = 0.8 - 0.6 * math.exp(-0.3 * 1)
    w_qkv = c_w_in[0].astype(BF16)
    pos = jnp.arange(N_META + seq, dtype=jnp.int32)
    rope_meta = _rope_tables(pos[:N_META], dqk, rot_dim)
    rope_real = _rope_tables(pos[N_META:], dqk, rot_dim)
    qkv_meta = _norm_proj(h_meta, norm_mix[1], [w_qkv], tm=N_META, tn=512, rope=rope_meta, rope_cols=d)
    def meta_block(a):
        a = a.reshape(N_META, c_heads, dv).transpose(1, 0, 2)
        return jnp.pad(a, ((0, 0), (0, LANES - N_META), (0, 0)))

    k_meta = meta_block(qkv_meta[:, d:2 * d])
    v_meta = meta_block(qkv_meta[:, 2 * d:])
    qkv = _norm_proj(h_real, norm_mix[1], [w_qkv], tm=_pick_tile(seq, 512), tn=512, rope=rope_real, rope_cols=d)
    o = _diff_attn(qkv, k_meta, v_meta, c_lambda[0].astype(F32), c_subln[0].astype(F32), nb=bsz, seq=seq,
                   heads=c_heads, tq=tq, hp=2, lambda_init=lambda_init)
    out = _mix_mlp(h_real, [o], c_w_out[0].astype(BF16), norm_mlp[1], w1[1], w2[1], norm_final,
                   tm=tm_mlp, tf=tf)
    return out.reshape(bsz, seq, d)
```
